```python
import jax, jax.numpy as jnp
from jax import lax
import numpy as np


D_MODEL = 2048
BATCH = 2
SEQ = 4096
DEPTH = 2

D_MIX = D_MODEL
D_FOX = D_MIX // 2
D_HGRN = D_MIX - D_FOX
FOX_HEADS = 8
FOX_HEAD_DIM = D_FOX // FOX_HEADS
HGRN_EXPAND = 128
HGRN_HEADS = D_HGRN // HGRN_EXPAND
HGRN_DV = D_HGRN // HGRN_HEADS
D_FF = 5632
CONV_WIDTH = 3
PLE_DIM = 256
Q_BLOCK = 128
CHUNK = 64
EPS = 1e-6

OFF_FOX_K = D_FOX
OFF_FOX_V = 2 * D_FOX
OFF_FOX_F = 3 * D_FOX
OFF_HG_Q = OFF_FOX_F + FOX_HEADS
OFF_HG_F = OFF_HG_Q + D_HGRN
OFF_HG_I = OFF_HG_F + D_HGRN
OFF_HG_G = OFF_HG_I + D_HGRN
N_IN = OFF_HG_G + D_HGRN

kernel_name = 'fox_hgrn2_parallel_hybrid_block'


def rms_norm(x, g):
    xf = x.astype(jnp.float32)
    y = xf * lax.rsqrt(jnp.mean(xf * xf, axis=-1, keepdims=True) + EPS)
    return (y * g.astype(jnp.float32)).astype(x.dtype)


def forgetting_attention(q, k, v, f_logit):
    b, s, h, dh = q.shape
    scale = dh ** -0.5
    c = jnp.cumsum(jax.nn.log_sigmoid(f_logit.astype(jnp.float32)), axis=1).transpose(0, 2, 1)
    qh = q.transpose(0, 2, 1, 3)
    kh = k.transpose(0, 2, 1, 3)
    vh = v.transpose(0, 2, 1, 3)
    outs = []
    for blk in range(s // Q_BLOCK):
        q0 = blk * Q_BLOCK
        q1 = q0 + Q_BLOCK
        logits = jnp.einsum('bhqd,bhkd->bhqk', qh[:, :, q0:q1], kh[:, :, :q1]).astype(jnp.float32) * scale
        logits = logits + c[:, :, q0:q1, None] - c[:, :, None, :q1]
        causal = (q0 + jnp.arange(Q_BLOCK))[:, None] >= jnp.arange(q1)[None, :]
        probs = jax.nn.softmax(jnp.where(causal, logits, -jnp.inf), axis=-1)
        outs.append(jnp.einsum('bhqk,bhkd->bhqd', probs.astype(v.dtype), vh[:, :, :q1]))
    o = jnp.concatenate(outs, axis=2)
    return o.transpose(0, 2, 1, 3).reshape(b, s, h * dh)


def hgrn2_recurrence(q, k, v, log_f):
    b, s, h, dk = q.shape
    dv = v.shape[-1]
    n = s // CHUNK

    def to_chunks(t):
        return t.astype(jnp.float32).reshape(b, n, CHUNK, h, t.shape[-1]).transpose(1, 0, 3, 2, 4)

    tri = jnp.arange(CHUNK)[:, None] >= jnp.arange(CHUNK)[None, :]

    def step(state, xs):
        qc, kc, vc, gc = xs
        cum = jnp.cumsum(gc, axis=2)
        o_inter = jnp.einsum('bhtk,bhkv->bhtv', qc * jnp.exp(cum), state)
        rel = jnp.where(tri[:, :, None], cum[:, :, :, None, :] - cum[:, :, None, :, :], -jnp.inf)
        scores = jnp.einsum('bhtk,bhsk,bhtsk->bhts', qc, kc, jnp.exp(rel))
        o_intra = jnp.einsum('bhts,bhsv->bhtv', scores, vc)
        last = cum[:, :, -1:, :]
        new_state = jnp.exp(last[:, :, 0, :])[..., None] * state + jnp.einsum('bhsk,bhsv->bhkv', kc * jnp.exp(last - cum), vc)
        return new_state, o_inter + o_intra

    state0 = jnp.zeros((b, h, dk, dv), jnp.float32)
    _, o = lax.scan(step, state0, (to_chunks(q), to_chunks(k), to_chunks(v), to_chunks(log_f)))
    return o.transpose(1, 0, 3, 2, 4).reshape(b, s, h, dv).astype(q.dtype)


def hgrn_lower_bounds(w_lb):
    cum = jnp.cumsum(jax.nn.softmax(w_lb.astype(jnp.float32), axis=0), axis=0)
    return cum - cum[0:1]


def token_mixer(h, w_in, b_f, lb, g_o, w_out):
    b, s, _ = h.shape
    z = h @ w_in
    fox_shape = (b, s, FOX_HEADS, FOX_HEAD_DIM)
    q_a = z[..., :OFF_FOX_K].reshape(fox_shape)
    k_a = z[..., OFF_FOX_K:OFF_FOX_V].reshape(fox_shape)
    v_a = z[..., OFF_FOX_V:OFF_FOX_F].reshape(fox_shape)
    f_a = z[..., OFF_FOX_F:OFF_HG_Q] + b_f
    y_fox = forgetting_attention(q_a, k_a, v_a, f_a)

    hg_shape = (b, s, HGRN_HEADS, HGRN_EXPAND)
    hv_shape = (b, s, HGRN_HEADS, HGRN_DV)
    q_b = z[..., OFF_HG_Q:OFF_HG_F]
    f_b = z[..., OFF_HG_F:OFF_HG_I].astype(jnp.float32)
    i_b = z[..., OFF_HG_I:OFF_HG_G]
    g_b = z[..., OFF_HG_G:]
    log_f = jnp.logaddexp(jnp.log(lb), jnp.log1p(-lb) + jax.nn.log_sigmoid(f_b))
    k_b = (1.0 - lb) * jax.nn.sigmoid(-f_b)
    o_b = hgrn2_recurrence(q_b.reshape(hg_shape), k_b.reshape(hg_shape), i_b.reshape(hv_shape), log_f.reshape(hg_shape))
    o_b = rms_norm(o_b, g_o) * jax.nn.silu(g_b.reshape(hv_shape))
    y = jnp.concatenate([y_fox, o_b.reshape(b, s, D_HGRN)], axis=-1)
    return y @ w_out


def conv_ffn(h, w_up, conv_w, conv_b, w_down):
    s = h.shape[1]
    u = h @ w_up
    u_pad = jnp.pad(u, ((0, 0), (CONV_WIDTH - 1, 0), (0, 0)))
    u = sum(conv_w[j] * u_pad[:, j:j + s] for j in range(CONV_WIDTH)) + conv_b
    gate, up = jnp.split(u, 2, axis=-1)
    return (jax.nn.silu(gate) * up) @ w_down


def setup_inputs(seed: int = 0) -> dict:
    key = jax.random.key(seed)
    ks = jax.random.split(key, 20)
    f32 = jnp.float32

    def nrm(k, shape, fan_in):
        return jax.random.normal(k, shape, f32) * fan_in ** -0.5

    def gain(k, shape):
        return 1.0 + 0.01 * jax.random.normal(k, shape, f32)

    return {
        'x': jax.random.normal(ks[0], (BATCH, SEQ, D_MODEL), f32),
        'p': jax.random.normal(ks[1], (DEPTH, BATCH, SEQ, PLE_DIM), f32),
        'g_mix_pre': gain(ks[2], (DEPTH, D_MODEL)),
        'w_in': nrm(ks[3], (DEPTH, D_MODEL, N_IN), D_MODEL),
        'b_fox_f': jax.random.uniform(ks[4], (DEPTH, FOX_HEADS), f32, minval=1.0, maxval=4.0),
        'w_hgrn_lb': 0.1 * jax.random.normal(ks[5], (DEPTH, D_HGRN), f32),
        'g_hgrn_out': gain(ks[6], (DEPTH, HGRN_DV)),
        'w_out': nrm(ks[7], (DEPTH, D_MIX, D_MODEL), D_MIX),
        'g_mix_post': gain(ks[8], (DEPTH, D_MODEL)),
        'g_ffn_pre': gain(ks[9], (DEPTH, D_MODEL)),
        'w_up': nrm(ks[10], (DEPTH, D_MODEL, 2 * D_FF), D_MODEL),
        'conv_w': nrm(ks[11], (DEPTH, CONV_WIDTH, 2 * D_FF), CONV_WIDTH),
        'conv_b': 0.01 * jax.random.normal(ks[12], (DEPTH, 2 * D_FF), f32),
        'w_down': nrm(ks[13], (DEPTH, D_FF, D_MODEL), D_FF),
        'g_ffn_post': gain(ks[14], (DEPTH, D_MODEL)),
        'g_ple_in': gain(ks[15], (DEPTH, D_MODEL)),
        'w_ple_gate': nrm(ks[16], (DEPTH, D_MODEL, D_MODEL), D_MODEL),
        'w_ple_proj': nrm(ks[17], (DEPTH, PLE_DIM, D_MODEL), PLE_DIM),
        'g_ple_post': gain(ks[18], (DEPTH, D_MODEL)),
    }


def reference(x, p, g_mix_pre, w_in, b_fox_f, w_hgrn_lb, g_hgrn_out, w_out, g_mix_post,
              g_ffn_pre, w_up, conv_w, conv_b, w_down, g_ffn_post,
              g_ple_in, w_ple_gate, w_ple_proj, g_ple_post):
    lbs = hgrn_lower_bounds(w_hgrn_lb)
    h = x
    for i in range(DEPTH):
        mix = token_mixer(rms_norm(h, g_mix_pre[i]), w_in[i], b_fox_f[i], lbs[i], g_hgrn_out[i], w_out[i])
        h = h + rms_norm(mix, g_mix_post[i])
        ff = conv_ffn(rms_norm(h, g_ffn_pre[i]), w_up[i], conv_w[i], conv_b[i], w_down[i])
        h = h + rms_norm(ff, g_ffn_post[i])
        e = p[i] @ w_ple_proj[i]
        gate = jax.nn.sigmoid(rms_norm(h, g_ple_in[i]) @ w_ple_gate[i])
        h = h + rms_norm(e * gate, g_ple_post[i])
    return h
```

```python
import functools

import jax
import jax.numpy as jnp
from jax import lax
from jax.experimental import pallas as pl
from jax.experimental.pallas import tpu as pltpu

F32 = jnp.float32
BF16 = jnp.bfloat16

D_MODEL = 2048
D_FOX = 1024
D_HGRN = 1024
HEADS = 8
HEAD_DIM = 128
D_FF = 5632
CONV_WIDTH = 3
EPS = 1e-6

HGRN_CHUNK = 64
ATT_BLOCK = 512

VMEM_LIMIT = 56 * 1024 * 1024


def _params(*semantics):
    return pltpu.CompilerParams(dimension_semantics=semantics,
                                vmem_limit_bytes=VMEM_LIMIT)


def _rms(x, g):
    ms = jnp.mean(x * x, axis=-1, keepdims=True)
    return (x * lax.rsqrt(ms + EPS)) * g


def _log_sigmoid(x):
    return jnp.minimum(x, 0.0) - jnp.log1p(jnp.exp(-jnp.abs(x)))


def _prenorm_kernel(x_ref, g_ref, o_ref):
    o_ref[...] = _rms(x_ref[...], g_ref[...]).astype(o_ref.dtype)


def _prenorm(x, g, tm=512):
    t, d = x.shape
    return pl.pallas_call(
        _prenorm_kernel,
        grid=(t // tm,),
        in_specs=[pl.BlockSpec((tm, d), lambda i: (i, 0)),
                  pl.BlockSpec((1, d), lambda i: (0, 0))],
        out_specs=pl.BlockSpec((tm, d), lambda i: (i, 0)),
        out_shape=jax.ShapeDtypeStruct((t, d), BF16),
        compiler_params=_params("arbitrary"),
        name="prenorm",
    )(x, g.reshape(1, d))


def _proj_kernel(x_ref, w_ref, s_ref, o_ref):
    acc = jnp.dot(x_ref[...], w_ref[...], preferred_element_type=F32)
    o_ref[...] = (acc * s_ref[...]).astype(o_ref.dtype)


def _proj(x, w, colscale, out_dtype, tm=1024, tn=1024, name="proj"):
    t, k = x.shape
    n = w.shape[1]
    return pl.pallas_call(
        _proj_kernel,
        grid=(n // tn, t // tm),
        in_specs=[pl.BlockSpec((tm, k), lambda j, i: (i, 0)),
                  pl.BlockSpec((k, tn), lambda j, i: (0, j)),
                  pl.BlockSpec((1, tn), lambda j, i: (0, j))],
        out_specs=pl.BlockSpec((tm, tn), lambda j, i: (i, j)),
        out_shape=jax.ShapeDtypeStruct((t, n), out_dtype),
        compiler_params=_params("arbitrary", "arbitrary"),
        name=name,
    )(x, w, colscale)


def _fgate_kernel(x_ref, w_ref, b_ref, o_ref, carry_ref, *, tm):
    @pl.when(pl.program_id(1) == 0)
    def _():
        carry_ref[...] = jnp.zeros_like(carry_ref)

    f = jnp.dot(x_ref[...], w_ref[...], preferred_element_type=F32) + b_ref[...]
    ls = _log_sigmoid(f)
    c = ls.T[:HEADS, :]
    lane = lax.broadcasted_iota(jnp.int32, c.shape, 1)
    d = 1
    while d < tm:
        c = c + jnp.where(lane >= d, pltpu.roll(c, d, 1), 0.0)
        d *= 2
    c = c + carry_ref[:, 0:1]
    o_ref[0] = c
    carry_ref[...] = jnp.broadcast_to(c[:, tm - 1:tm], carry_ref.shape)


def _fgate(xn, w_f, b_f, batch, seq, tm=512):
    t, k = xn.shape
    nt = seq // tm
    return pl.pallas_call(
        functools.partial(_fgate_kernel, tm=tm),
        grid=(batch, nt),
        in_specs=[pl.BlockSpec((tm, k), lambda b, s: (b * nt + s, 0)),
                  pl.BlockSpec((k, 128), lambda b, s: (0, 0)),
                  pl.BlockSpec((1, 128), lambda b, s: (0, 0))],
        out_specs=pl.BlockSpec((1, HEADS, tm), lambda b, s: (b, 0, s)),
        out_shape=jax.ShapeDtypeStruct((batch, HEADS, seq), F32),
        scratch_shapes=[pltpu.VMEM((HEADS, 128), F32)],
        compiler_params=_params("arbitrary", "arbitrary"),
        name="fgate",
    )(xn, w_f, b_f)


def _attn_kernel(q_ref, k_ref, v_ref, c_ref, o_ref, *, blk):
    qi = pl.program_id(2)
    q = q_ref[0]

    def scores(kb):
        ks = pl.multiple_of(kb * blk, blk)
        k = k_ref[0, pl.ds(ks, blk), :]
        s = lax.dot_general(q, k, (((1,), (1,)), ((), ())),
                            preferred_element_type=F32)
        return s - c_ref[kb], ks

    def update(s, ks, carry):
        m, l, acc = carry
        m_new = jnp.maximum(m, jnp.max(s, axis=-1, keepdims=True))
        alpha = jnp.exp(m - m_new)
        p = jnp.exp(s - m_new)
        l = alpha * l + jnp.sum(p, axis=-1, keepdims=True)
        v = v_ref[0, pl.ds(ks, blk), :]
        acc = alpha * acc + jnp.dot(p.astype(BF16), v, preferred_element_type=F32)
        return m_new, l, acc

    def body(kb, carry):
        s, ks = scores(kb)
        return update(s, ks, carry)

    init = (jnp.full((blk, 1), -jnp.inf, F32), jnp.zeros((blk, 1), F32),
            jnp.zeros((blk, HEAD_DIM), F32))
    carry = lax.fori_loop(0, qi, body, init)

    s, ks = scores(qi)
    row = lax.broadcasted_iota(jnp.int32, s.shape, 0)
    col = lax.broadcasted_iota(jnp.int32, s.shape, 1)
    s = jnp.where(row >= col, s, -jnp.inf)
    _, l, acc = update(s, ks, carry)
    o_ref[0] = (acc / l).astype(o_ref.dtype)


def _attention(qkv, c, batch, seq, blk=ATT_BLOCK):
    nb = seq // blk
    c_blocks = c.reshape(batch * HEADS * nb, 1, blk)
    return pl.pallas_call(
        functools.partial(_attn_kernel, blk=blk),
        grid=(batch, HEADS, nb),
        in_specs=[
            pl.BlockSpec((1, blk, HEAD_DIM), lambda b, h, i: (b, i, h)),
            pl.BlockSpec((1, seq, HEAD_DIM), lambda b, h, i: (b, 0, HEADS + h)),
            pl.BlockSpec((1, seq, HEAD_DIM), lambda b, h, i: (b, 0, 2 * HEADS + h)),
            pl.BlockSpec((nb, 1, blk), lambda b, h, i: (b * HEADS + h, 0, 0)),
        ],
        out_specs=pl.BlockSpec((1, blk, HEAD_DIM), lambda b, h, i: (b, i, h)),
        out_shape=jax.ShapeDtypeStruct((batch, seq, D_FOX), BF16),
        compiler_params=_params("arbitrary", "arbitrary", "arbitrary"),
        name="fox_attention",
    )(qkv, qkv, qkv, c_blocks)


def _level_exponents(cum, chunk):
    row = lax.broadcasted_iota(jnp.int32, cum.shape, 0)
    out = []
    h = chunk // 2
    while h >= 8:
        blocks = cum.reshape(chunk // (2 * h), 2 * h, cum.shape[-1])
        r = jnp.broadcast_to(blocks[:, h - 1:h, :], blocks.shape).reshape(cum.shape)
        is_q = (row % (2 * h)) >= h
        out.append((h, jnp.where(is_q, cum - r, r - cum), is_q))
        h //= 2
    down = {d: pltpu.roll(cum, d, 0) for d in (1, 2, 3, 4)}
    up = {d: pltpu.roll(cum, cum.shape[0] - d, 0) for d in (1, 2, 3)}
    while h >= 1:
        j = row % (2 * h)
        e = jnp.zeros_like(cum)
        for jj in range(2 * h):
            if jj >= h:
                val = cum - down[jj - h + 1]
            elif h - 1 - jj > 0:
                val = up[h - 1 - jj] - cum
            else:
                continue
            e = jnp.where(j == jj, val, e)
        out.append((h, e, j >= h))
        h //= 2
    return out


def _hgrn_kernel(q_ref, f_ref, i_ref, g_ref, wlb_ref, go_ref, o_ref, state_ref,
                 *, chunk, layer):
    @pl.when(pl.program_id(1) == 0)
    def _():
        state_ref[...] = jnp.zeros_like(state_ref)

    w = wlb_ref[...]
    ew = jnp.exp(w - jnp.max(w, axis=0, keepdims=True))
    sm = ew / jnp.sum(ew, axis=0, keepdims=True)
    lb_all = jnp.zeros_like(sm[0:1])
    for dpt in range(1, layer + 1):
        lb_all = lb_all + sm[dpt:dpt + 1]
    log_lb_all = jnp.log(lb_all)
    log_1mlb_all = jnp.log1p(-lb_all)

    row = lax.broadcasted_iota(jnp.int32, (chunk, HEAD_DIM), 0)
    rr = lax.broadcasted_iota(jnp.int32, (chunk, chunk), 0)
    cc = lax.broadcasted_iota(jnp.int32, (chunk, chunk), 1)

    for hd in range(HEADS):
        sl = slice(hd * HEAD_DIM, (hd + 1) * HEAD_DIM)
        q = q_ref[:, sl]
        f = f_ref[:, sl]
        v = i_ref[:, sl]
        lb = lb_all[:, sl]
        a = log_lb_all[:, sl]
        b = log_1mlb_all[:, sl] + _log_sigmoid(f)
        log_f = jnp.maximum(a, b) + jnp.log1p(jnp.exp(-jnp.abs(a - b)))
        k = (1.0 - lb) * (1.0 / (1.0 + jnp.exp(f)))

        cum = log_f
        d = 1
        while d < chunk:
            cum = cum + jnp.where(row >= d, pltpu.roll(cum, d, 0), 0.0)
            d *= 2
        last = cum[chunk - 1:chunk, :]

        v16 = v.astype(BF16)
        sc = jnp.zeros((chunk, chunk), F32)
        for h, e, is_q in _level_exponents(cum, chunk):
            decay = jnp.exp(e)
            qm = jnp.where(is_q, q * decay, 0.0).astype(BF16)
            km = jnp.where(is_q, 0.0, k * decay).astype(BF16)
            s_l = lax.dot_general(qm, km, (((1,), (1,)), ((), ())),
                                  preferred_element_type=F32)
            same_block = (rr // (2 * h)) == (cc // (2 * h))
            sc = sc + jnp.where(same_block, s_l, 0.0)
        o = jnp.dot(sc.astype(BF16), v16, preferred_element_type=F32)
        o = o + jnp.sum(q * k, axis=-1, keepdims=True) * v

        st = state_ref[hd]
        qd = (q * jnp.exp(cum)).astype(BF16)
        o = o + lax.dot_general(qd, st.astype(BF16), (((1,), (1,)), ((), ())),
                                preferred_element_type=F32)
        kd = (k * jnp.exp(last - cum)).astype(BF16)
        upd = lax.dot_general(v16, kd, (((0,), (0,)), ((), ())),
                              preferred_element_type=F32)
        state_ref[hd] = st * jnp.exp(last) + upd

        g = g_ref[:, sl]
        y = _rms(o, go_ref[...]) * (g * (1.0 / (1.0 + jnp.exp(-g))))
        o_ref[:, sl] = y.astype(o_ref.dtype)


def _hgrn(zh, w_lb, g_o, layer, batch, seq, chunk=HGRN_CHUNK):
    t = zh.shape[0]
    nc = seq // chunk
    spec = lambda grp: pl.BlockSpec((chunk, D_HGRN), lambda b, c: (b * nc + c, grp))
    return pl.pallas_call(
        functools.partial(_hgrn_kernel, chunk=chunk, layer=layer),
        grid=(batch, nc),
        in_specs=[spec(0), spec(1), spec(2), spec(3),
                  pl.BlockSpec(w_lb.shape, lambda b, c: (0, 0)),
                  pl.BlockSpec((1, HEAD_DIM), lambda b, c: (0, 0))],
        out_specs=pl.BlockSpec((chunk, D_HGRN), lambda b, c: (b * nc + c, 0)),
        out_shape=jax.ShapeDtypeStruct((t, D_HGRN), BF16),
        scratch_shapes=[pltpu.VMEM((HEADS, HEAD_DIM, HEAD_DIM), F32)],
        compiler_params=_params("arbitrary", "arbitrary"),
        name="hgrn2",
    )(zh, zh, zh, zh, w_lb, g_o.reshape(1, HEAD_DIM))


def _outproj_kernel(ya_ref, yb_ref, wa_ref, wb_ref, h_ref, gpost_ref, gnext_ref,
                    h_out_ref, hn_out_ref):
    mix = jnp.dot(ya_ref[...], wa_ref[...], preferred_element_type=F32)
    mix = mix + jnp.dot(yb_ref[...], wb_ref[...], preferred_element_type=F32)
    h = h_ref[...] + _rms(mix, gpost_ref[...])
    h_out_ref[...] = h
    hn_out_ref[...] = _rms(h, gnext_ref[...]).astype(hn_out_ref.dtype)


def _outproj(ya, yb, w_out, h, g_post, g_next, tm=256):
    t, d = h.shape
    ka, kb = ya.shape[1], yb.shape[1]
    row = lambda width: pl.BlockSpec((tm, width), lambda i: (i, 0))
    fixed = lambda r, blk: pl.BlockSpec((r, d), lambda i: (blk, 0))
    return pl.pallas_call(
        _outproj_kernel,
        grid=(t // tm,),
        in_specs=[row(ka), row(kb), fixed(ka, 0), fixed(kb, 1), row(d),
                  fixed(1, 0), fixed(1, 0)],
        out_specs=[row(d), row(d)],
        out_shape=[jax.ShapeDtypeStruct((t, d), F32), jax.ShapeDtypeStruct((t, d), BF16)],
        compiler_params=_params("arbitrary"),
        name="outproj",
    )(ya, yb, w_out, w_out, h, g_post.reshape(1, d), g_next.reshape(1, d))


def _upproj_kernel(x_ref, wg_ref, wu_ref, cwg_ref, cwu_ref, cbg_ref, cbu_ref, o_ref,
                   ug_ref, uu_ref, *, tm, tiles_per_seq):
    i = pl.program_id(1)

    @pl.when(i % tiles_per_seq == 0)
    def _():
        ug_ref[0:8, :] = jnp.zeros((8, ug_ref.shape[1]), F32)
        uu_ref[0:8, :] = jnp.zeros((8, uu_ref.shape[1]), F32)

    x = x_ref[...]
    ug_ref[8:tm + 8, :] = jnp.dot(x, wg_ref[...], preferred_element_type=F32)
    uu_ref[8:tm + 8, :] = jnp.dot(x, wu_ref[...], preferred_element_type=F32)

    def conv(u_ref, cw_ref, cb_ref):
        acc = cb_ref[...] + cw_ref[2:3, :] * u_ref[8:tm + 8, :]
        acc = acc + cw_ref[1:2, :] * u_ref[7:tm + 7, :]
        return acc + cw_ref[0:1, :] * u_ref[6:tm + 6, :]

    gate = conv(ug_ref, cwg_ref, cbg_ref)
    up = conv(uu_ref, cwu_ref, cbu_ref)
    o_ref[...] = (gate * (1.0 / (1.0 + jnp.exp(-gate))) * up).astype(o_ref.dtype)

    ug_ref[0:8, :] = ug_ref[tm:tm + 8, :]
    uu_ref[0:8, :] = uu_ref[tm:tm + 8, :]


def _upproj(xn, w_up, conv_w, conv_b, seq, tm=1024, tn=512):
    t, k = xn.shape
    nj = D_FF // tn
    return pl.pallas_call(
        functools.partial(_upproj_kernel, tm=tm, tiles_per_seq=seq // tm),
        grid=(nj, t // tm),
        in_specs=[pl.BlockSpec((tm, k), lambda j, i: (i, 0)),
                  pl.BlockSpec((k, tn), lambda j, i: (0, j)),
                  pl.BlockSpec((k, tn), lambda j, i: (0, j + nj)),
                  pl.BlockSpec((CONV_WIDTH, tn), lambda j, i: (0, j)),
                  pl.BlockSpec((CONV_WIDTH, tn), lambda j, i: (0, j + nj)),
                  pl.BlockSpec((1, tn), lambda j, i: (0, j)),
                  pl.BlockSpec((1, tn), lambda j, i: (0, j + nj))],
        out_specs=pl.BlockSpec((tm, tn), lambda j, i: (i, j)),
        out_shape=jax.ShapeDtypeStruct((t, D_FF), BF16),
        scratch_shapes=[pltpu.VMEM((tm + 8, tn), F32), pltpu.VMEM((tm + 8, tn), F32)],
        compiler_params=_params("arbitrary", "arbitrary"),
        name="upproj_conv",
    )(xn, w_up, w_up, conv_w, conv_w, conv_b, conv_b)


def _downproj_kernel(a_ref, w_ref, h_ref, gpost_ref, gnext_ref, h_out_ref, hn_out_ref):
    ff = jnp.dot(a_ref[...], w_ref[...], preferred_element_type=F32)
    h = h_ref[...] + _rms(ff, gpost_ref[...])
    h_out_ref[...] = h
    hn_out_ref[...] = _rms(h, gnext_ref[...]).astype(hn_out_ref.dtype)


def _downproj(a, w_down, h, g_post, g_next, tm=256):
    t, d = h.shape
    k = a.shape[1]
    row = lambda width: pl.BlockSpec((tm, width), lambda i: (i, 0))
    fixed = lambda r: pl.BlockSpec((r, d), lambda i: (0, 0))
    return pl.pallas_call(
        _downproj_kernel,
        grid=(t // tm,),
        in_specs=[row(k),
                  pl.BlockSpec((k, d), lambda i: (0, 0), pipeline_mode=pl.Buffered(1)),
                  row(d), fixed(1), fixed(1)],
        out_specs=[row(d), row(d)],
        out_shape=[jax.ShapeDtypeStruct((t, d), F32), jax.ShapeDtypeStruct((t, d), BF16)],
        compiler_params=_params("arbitrary"),
        name="downproj",
    )(a, w_down, h, g_post.reshape(1, d), g_next.reshape(1, d))


def _ple_kernel(p_ref, hn_ref, wp_ref, wg_ref, h_ref, gpost_ref, gnext_ref,
                h_out_ref, hn_out_ref):
    e = jnp.dot(p_ref[...].astype(BF16), wp_ref[...], preferred_element_type=F32)
    z = jnp.dot(hn_ref[...], wg_ref[...], preferred_element_type=F32)
    gate = 1.0 / (1.0 + jnp.exp(-z))
    h = h_ref[...] + _rms(e * gate, gpost_ref[...])
    h_out_ref[...] = h
    hn_out_ref[...] = _rms(h, gnext_ref[...]).astype(hn_out_ref.dtype)


def _ple(p, hn, w_proj, w_gate, h, g_post, g_next, tm=256):
    t, d = h.shape
    kp = p.shape[1]
    row = lambda width: pl.BlockSpec((tm, width), lambda i: (i, 0))
    fixed = lambda r: pl.BlockSpec((r, d), lambda i: (0, 0))
    return pl.pallas_call(
        _ple_kernel,
        grid=(t // tm,),
        in_specs=[row(kp), row(d), fixed(kp), fixed(d), row(d), fixed(1), fixed(1)],
        out_specs=[row(d), row(d)],
        out_shape=[jax.ShapeDtypeStruct((t, d), F32), jax.ShapeDtypeStruct((t, d), BF16)],
        compiler_params=_params("arbitrary"),
        name="ple",
    )(p, hn, w_proj, w_gate, h, g_post.reshape(1, d), g_next.reshape(1, d))


def kernel(x, p, g_mix_pre, w_in, b_fox_f, w_hgrn_lb, g_hgrn_out, w_out, g_mix_post,
           g_ffn_pre, w_up, conv_w, conv_b, w_down, g_ffn_post,
           g_ple_in, w_ple_gate, w_ple_proj, g_ple_post):
    batch, seq, d = x.shape
    depth = w_in.shape[0]
    t = batch * seq
    off_f = 3 * D_FOX
    off_hg = off_f + HEADS

    qscale = jnp.concatenate([jnp.full((1, D_FOX), HEAD_DIM ** -0.5, F32),
                              jnp.ones((1, 2 * D_FOX), F32)], axis=1)
    ones_hg = jnp.ones((1, 4 * D_HGRN), F32)

    h = x.reshape(t, d)
    hn = _prenorm(h, g_mix_pre[0])
    for i in range(depth):
        w_qkv = w_in[i, :, :off_f].astype(BF16)
        w_f = jnp.pad(w_in[i, :, off_f:off_hg], ((0, 0), (0, 128 - HEADS))).astype(BF16)
        b_f = jnp.pad(b_fox_f[i], (0, 128 - HEADS)).reshape(1, 128)
        w_hg = w_in[i, :, off_hg:].astype(BF16)

        qkv = _proj(hn, w_qkv, qscale, BF16, name="inproj_qkv")
        zh = _proj(hn, w_hg, ones_hg, F32, name="inproj_hgrn")
        c = _fgate(hn, w_f, b_f, batch, seq)
        y_fox = _attention(qkv.reshape(batch, seq, 3 * D_FOX), c, batch, seq)
        y_hg = _hgrn(zh, w_hgrn_lb, g_hgrn_out[i], i, batch, seq)

        h, hn = _outproj(y_fox.reshape(t, D_FOX), y_hg, w_out[i].astype(BF16), h,
                         g_mix_post[i], g_ffn_pre[i])
        a = _upproj(hn, w_up[i].astype(BF16), conv_w[i], conv_b[i].reshape(1, -1), seq)
        h, hn = _downproj(a, w_down[i].astype(BF16), h, g_ffn_post[i], g_ple_in[i])
        g_next = g_mix_pre[i + 1] if i + 1 < depth else g_mix_pre[i]
        h, hn = _ple(p[i].reshape(t, -1), hn, w_ple_proj[i].astype(BF16),
                     w_ple_gate[i].astype(BF16), h, g_ple_post[i], g_next)
    return h.reshape(batch, seq, d)
```

```python
import functools
import math

import jax
import jax.numpy as jnp
from jax import lax
from jax.experimental import pallas as pl
from jax.experimental.pallas import tpu as pltpu

F32 = jnp.float32
BF16 = jnp.bfloat16

D_MODEL = 2048
D_FOX = 1024
D_HGRN = 1024
HEADS = 8
HEAD_DIM = 128
D_FF = 5632
CONV_WIDTH = 3
EPS = 1e-6
LOG2E = math.log2(math.e)

HGRN_CHUNK = 64
ATT_BLOCK = 512

VMEM_LIMIT = 56 * 1024 * 1024


def _params(*semantics):
    return pltpu.CompilerParams(dimension_semantics=semantics,
                                vmem_limit_bytes=VMEM_LIMIT)


def _rms(x, g):
    ms = jnp.mean(x * x, axis=-1, keepdims=True)
    return (x * lax.rsqrt(ms + EPS)) * g


def _log_sigmoid(x):
    return jnp.minimum(x, 0.0) - jnp.log(1.0 + jnp.exp(-jnp.abs(x)))


def _resident(shape, index_map):
    return pl.BlockSpec(shape, index_map, pipeline_mode=pl.Buffered(1))


def _prenorm_kernel(x_ref, g_ref, o_ref):
    o_ref[...] = _rms(x_ref[...], g_ref[...]).astype(o_ref.dtype)


def _prenorm(x, g, tm=512):
    t, d = x.shape
    return pl.pallas_call(
        _prenorm_kernel,
        grid=(t // tm,),
        in_specs=[pl.BlockSpec((tm, d), lambda i: (i, 0)),
                  pl.BlockSpec((1, d), lambda i: (0, 0))],
        out_specs=pl.BlockSpec((tm, d), lambda i: (i, 0)),
        out_shape=jax.ShapeDtypeStruct((t, d), BF16),
        compiler_params=_params("arbitrary"),
        name="prenorm",
    )(x, g.reshape(1, d))


def _proj_kernel(x_ref, w_ref, s_ref, o_ref, wb_ref):
    @pl.when(pl.program_id(1) == 0)
    def _():
        wb_ref[...] = w_ref[...].astype(BF16)

    acc = jnp.dot(x_ref[...], wb_ref[...], preferred_element_type=F32)
    o_ref[...] = (acc * s_ref[...]).astype(o_ref.dtype)


def _proj(x, w, layer, n, colscale, out_dtype, tm=1024, tn=1024, name="proj"):
    t, k = x.shape
    return pl.pallas_call(
        _proj_kernel,
        grid=(n // tn, t // tm),
        in_specs=[pl.BlockSpec((tm, k), lambda j, i: (i, 0)),
                  pl.BlockSpec((None, k, tn), lambda j, i: (layer, 0, j)),
                  pl.BlockSpec((1, tn), lambda j, i: (0, j))],
        out_specs=pl.BlockSpec((tm, tn), lambda j, i: (i, j)),
        out_shape=jax.ShapeDtypeStruct((t, n), out_dtype),
        scratch_shapes=[pltpu.VMEM((k, tn), BF16)],
        compiler_params=_params("arbitrary", "arbitrary"),
        name=name,
    )(x, w, colscale)


def _fgate_kernel(x_ref, w_ref, b_ref, o_ref, carry_ref, *, tm):
    @pl.when(pl.program_id(1) == 0)
    def _():
        carry_ref[...] = jnp.zeros_like(carry_ref)

    f = jnp.dot(x_ref[...], w_ref[...], preferred_element_type=F32) + b_ref[...]
    c = _log_sigmoid(f)
    row = lax.broadcasted_iota(jnp.int32, c.shape, 0)
    d = 1
    while d < tm:
        c = c + jnp.where(row >= d, pltpu.roll(c, d, 0), 0.0)
        d *= 2
    c = c + carry_ref[0:1, :]
    carry_ref[...] = jnp.broadcast_to(c[tm - 1:tm, :], carry_ref.shape)

    lane = lax.broadcasted_iota(jnp.int32, c.shape, 1)
    x = c * (-LOG2E)
    for hd in range(HEADS):
        col = jnp.broadcast_to(x[:, hd:hd + 1], x.shape)
        hi = col.astype(BF16).astype(F32)
        rest = col - hi
        mid = rest.astype(BF16).astype(F32)
        lo = rest - mid
        parts = jnp.where(lane == 0, hi, jnp.where(lane == 1, mid,
                          jnp.where(lane == 2, lo, 0.0)))
        o_ref[0, hd] = parts.astype(BF16)


def _fgate(xn, w_f, b_f, batch, seq, tm=512):
    t, k = xn.shape
    nt = seq // tm
    return pl.pallas_call(
        functools.partial(_fgate_kernel, tm=tm),
        grid=(batch, nt),
        in_specs=[pl.BlockSpec((tm, k), lambda b, s: (b * nt + s, 0)),
                  pl.BlockSpec((k, 128), lambda b, s: (0, 0)),
                  pl.BlockSpec((1, 128), lambda b, s: (0, 0))],
        out_specs=pl.BlockSpec((1, HEADS, tm, 128), lambda b, s: (b, 0, s, 0)),
        out_shape=jax.ShapeDtypeStruct((batch, HEADS, seq, 128), BF16),
        scratch_shapes=[pltpu.VMEM((8, 128), F32)],
        compiler_params=_params("arbitrary", "arbitrary"),
        name="fgate",
    )(xn, w_f, b_f)


def _attn_kernel(qa_ref, qb_ref, k_ref, v_ref, c_ref, o_ref, vt_ref, q_scr, s_ref,
                 bmax_ref, m_ref, acc_ref, *, blk, nb):
    ip = pl.program_id(2)

    @pl.when(ip == 0)
    def _():
        for hh in range(2):
            for r in range(nb):
                vblk = v_ref[0, r * blk:(r + 1) * blk, hh * HEAD_DIM:(hh + 1) * HEAD_DIM]
                vt_ref[hh * nb + r, 0:HEAD_DIM, :] = vblk.astype(F32).T.astype(BF16)
                vt_ref[hh * nb + r, HEAD_DIM:HEAD_DIM + 8, :] = jnp.ones((8, blk), BF16)

    m_ref[...] = jnp.full(m_ref.shape, -jnp.inf, F32)
    acc_ref[...] = jnp.zeros(acc_ref.shape, F32)

    lane = lax.broadcasted_iota(jnp.int32, (blk, HEAD_DIM), 1)
    ones3 = jnp.where(lane < 3, 1.0, 0.0).astype(BF16)
    for tile, q_ref in enumerate((qa_ref, qb_ref)):
        for hh in range(2):
            q_scr[2 * tile + hh] = jnp.concatenate(
                [q_ref[0, :, hh * HEAD_DIM:(hh + 1) * HEAD_DIM], ones3], axis=1)

    def item(t):
        if t == nb:
            return 1, nb - 1 - ip, True
        if t >= nb // 2:
            return 1, t - ip - 1, False
        in_b = t > ip
        return in_b.astype(jnp.int32), jnp.where(in_b, t - ip - 1, t), t == ip

    def scores(t):
        tile, kb, diag = item(t)
        ks = pl.multiple_of(kb * blk, blk)
        for hh in range(2):
            k_aug = jnp.concatenate(
                [k_ref[0, pl.ds(ks, blk), hh * HEAD_DIM:(hh + 1) * HEAD_DIM],
                 c_ref[0, hh, pl.ds(ks, blk), :]], axis=1)
            s = lax.dot_general(k_aug, q_scr[2 * tile + hh], (((1,), (1,)), ((), ())),
                                preferred_element_type=F32)
            if diag is not False:
                key = lax.broadcasted_iota(jnp.int32, s.shape, 0)
                qry = lax.broadcasted_iota(jnp.int32, s.shape, 1)
                keep = (key <= qry) if diag is True else ((key <= qry) | (t != ip))
                s = jnp.where(keep, s, -jnp.inf)
            bmax_ref[t % 2, hh] = jnp.max(s, axis=0, keepdims=True)
            s_ref[t % 2, hh] = s

    def softmax_pv(t):
        tile, kb, _ = item(t)
        for hh in range(2):
            idx = 2 * tile + hh
            m_old = m_ref[idx]
            m_new = jnp.maximum(m_old, bmax_ref[t % 2, hh])
            alpha = jnp.exp2(m_old - m_new)
            p = jnp.exp2(s_ref[t % 2, hh] - m_new)
            pv = jnp.dot(vt_ref[hh * nb + kb], p.astype(BF16), preferred_element_type=F32)
            acc_ref[idx] = alpha * acc_ref[idx] + pv
            m_ref[idx] = m_new

    scores(0)
    for t in range(nb + 1):
        if t < nb:
            scores(t + 1)
        softmax_pv(t)

    for tile in range(2):
        for hh in range(2):
            acc = acc_ref[2 * tile + hh]
            o = (acc[0:HEAD_DIM] / acc[HEAD_DIM:HEAD_DIM + 1]).T
            o_ref[0, 0, tile, :, hh * HEAD_DIM:(hh + 1) * HEAD_DIM] = o.astype(o_ref.dtype)


def _attention(qkv, c3, batch, seq, blk=ATT_BLOCK):
    nb = seq // blk
    pair = 2 * HEAD_DIM
    npair = HEADS // 2
    return pl.pallas_call(
        functools.partial(_attn_kernel, blk=blk, nb=nb),
        grid=(batch, npair, nb // 2),
        in_specs=[
            pl.BlockSpec((1, blk, pair), lambda b, h, i: (b, i, h)),
            pl.BlockSpec((1, blk, pair), lambda b, h, i: (b, nb - 1 - i, h)),
            pl.BlockSpec((1, seq, pair), lambda b, h, i: (b, 0, npair + h)),
            pl.BlockSpec((1, seq, pair), lambda b, h, i: (b, 0, 2 * npair + h)),
            pl.BlockSpec((1, 2, seq, 128), lambda b, h, i: (b, h, 0, 0)),
        ],
        out_specs=pl.BlockSpec((1, 1, 2, blk, pair), lambda b, h, i: (b, i, 0, 0, h)),
        out_shape=jax.ShapeDtypeStruct((batch, nb // 2, 2, blk, D_FOX), BF16),
        scratch_shapes=[pltpu.VMEM((2 * nb, HEAD_DIM + 8, blk), BF16),
                        pltpu.VMEM((4, blk, pair), BF16),
                        pltpu.VMEM((2, 2, blk, blk), F32),
                        pltpu.VMEM((2, 2, 1, blk), F32),
                        pltpu.VMEM((4, 1, blk), F32),
                        pltpu.VMEM((4, HEAD_DIM + 8, blk), F32)],
        compiler_params=_params("arbitrary", "arbitrary", "arbitrary"),
        name="fox_attention",
    )(qkv, qkv, qkv, qkv, c3)


def _block_middle_rows(cum, chunk):
    row = lax.broadcasted_iota(jnp.int32, cum.shape, 0)
    lanes = cum.shape[-1]
    out = []
    h = chunk // 2
    while h >= 4:
        blocks = cum.reshape(chunk // (2 * h), 2 * h, lanes)
        r = jnp.broadcast_to(blocks[:, h - 1:h, :], blocks.shape).reshape(cum.shape)
        out.append((h, r))
        h //= 2
    tiles = cum.reshape(chunk // 8, 8, lanes)
    pick = lambda j: jnp.broadcast_to(tiles[:, j:j + 1, :], tiles.shape).reshape(cum.shape)
    out.append((2, jnp.where(row % 8 < 4, pick(1), pick(5))))
    out.append((1, jnp.where(row % 2 == 1, pltpu.roll(cum, 1, 0), cum)))
    return out


def _hgrn_kernel(q_ref, f_ref, i_ref, g_ref, wlb_ref, go_ref, o_ref, state_ref,
                 *, chunk, layer):
    @pl.when(pl.program_id(1) == 0)
    def _():
        state_ref[...] = jnp.zeros_like(state_ref)

    if layer > 0:
        w = wlb_ref[...]
        ew = jnp.exp(w - jnp.max(w, axis=0, keepdims=True))
        sm = ew / jnp.sum(ew, axis=0, keepdims=True)
        lb_all = sm[1:2]
        for dpt in range(2, layer + 1):
            lb_all = lb_all + sm[dpt:dpt + 1]
        log_lb_all = jnp.log(lb_all)
        log_1mlb_all = jnp.log1p(-lb_all)

    row = lax.broadcasted_iota(jnp.int32, (chunk, HEAD_DIM), 0)
    rr = lax.broadcasted_iota(jnp.int32, (chunk, chunk), 0)
    cc = lax.broadcasted_iota(jnp.int32, (chunk, chunk), 1)
    pair_mask = {}
    h = chunk // 2
    while h >= 1:
        pair_mask[h] = (((rr // (2 * h)) == (cc // (2 * h)))
                        & (rr % (2 * h) >= h) & (cc % (2 * h) < h))
        h //= 2

    stage = []
    for hd in range(HEADS):
        sl = slice(hd * HEAD_DIM, (hd + 1) * HEAD_DIM)
        q = q_ref[:, sl]
        f = f_ref[:, sl]
        v = i_ref[:, sl]
        if layer == 0:
            log_f = _log_sigmoid(f)
            k = 1.0 / (1.0 + jnp.exp(f))
        else:
            a = log_lb_all[:, sl]
            b = log_1mlb_all[:, sl] + _log_sigmoid(f)
            log_f = jnp.maximum(a, b) + jnp.log(1.0 + jnp.exp(-jnp.abs(a - b)))
            k = (1.0 - lb_all[:, sl]) * (1.0 / (1.0 + jnp.exp(f)))

        cum = log_f * LOG2E
        d = 1
        while d < chunk:
            cum = cum + jnp.where(row >= d, pltpu.roll(cum, d, 0), 0.0)
            d *= 2
        last = cum[chunk - 1:chunk, :]

        sc = jnp.zeros((chunk, chunk), F32)
        for h, r in _block_middle_rows(cum, chunk):
            decay = jnp.exp2(-jnp.abs(cum - r))
            s_l = lax.dot_general((q * decay).astype(BF16), (k * decay).astype(BF16),
                                  (((1,), (1,)), ((), ())), preferred_element_type=F32)
            sc = sc + jnp.where(pair_mask[h], s_l, 0.0)
        stage.append((sc.astype(BF16), v.astype(BF16),
                      jnp.sum(q * k, axis=-1, keepdims=True) * v,
                      (q * jnp.exp2(cum)).astype(BF16),
                      (k * jnp.exp2(last - cum)).astype(BF16), jnp.exp2(last)))

    for hd in range(HEADS):
        sl = slice(hd * HEAD_DIM, (hd + 1) * HEAD_DIM)
        sc16, v16, o_diag, qd, kd, decay_last = stage[hd]
        st = state_ref[hd]
        o = o_diag + jnp.dot(sc16, v16, preferred_element_type=F32)
        o = o + lax.dot_general(qd, st.astype(BF16), (((1,), (1,)), ((), ())),
                                preferred_element_type=F32)
        upd = lax.dot_general(v16, kd, (((0,), (0,)), ((), ())),
                              preferred_element_type=F32)
        state_ref[hd] = st * decay_last + upd

        g = g_ref[:, sl]
        y = _rms(o, go_ref[...]) * (g * (1.0 / (1.0 + jnp.exp(-g))))
        o_ref[:, sl] = y.astype(o_ref.dtype)


def _hgrn(zh, w_lb, g_o, layer, batch, seq, chunk=HGRN_CHUNK):
    t = zh.shape[0]
    nc = seq // chunk
    spec = lambda grp: pl.BlockSpec((chunk, D_HGRN), lambda b, c: (b * nc + c, grp))
    return pl.pallas_call(
        functools.partial(_hgrn_kernel, chunk=chunk, layer=layer),
        grid=(batch, nc),
        in_specs=[spec(0), spec(1), spec(2), spec(3),
                  pl.BlockSpec(w_lb.shape, lambda b, c: (0, 0)),
                  pl.BlockSpec((1, HEAD_DIM), lambda b, c: (0, 0))],
        out_specs=pl.BlockSpec((chunk, D_HGRN), lambda b, c: (b * nc + c, 0)),
        out_shape=jax.ShapeDtypeStruct((t, D_HGRN), BF16),
        scratch_shapes=[pltpu.VMEM((HEADS, HEAD_DIM, HEAD_DIM), F32)],
        compiler_params=_params("arbitrary", "arbitrary"),
        name="hgrn2",
    )(zh, zh, zh, zh, w_lb, g_o.reshape(1, HEAD_DIM))


def _outproj_kernel(ya_ref, yb_ref, wa_ref, wb_ref, h_ref, gpost_ref, gnext_ref,
                    h_out_ref, hn_out_ref):
    mix = jnp.dot(ya_ref[...], wa_ref[...], preferred_element_type=F32)
    mix = mix + jnp.dot(yb_ref[...], wb_ref[...], preferred_element_type=F32)
    h = h_ref[...] + _rms(mix, gpost_ref[...])
    h_out_ref[...] = h
    hn_out_ref[...] = _rms(h, gnext_ref[...]).astype(hn_out_ref.dtype)


def _outproj(ya, yb, w_out, h, g_post, g_next):
    t, d = h.shape
    _, half_nb, _, tm, ka = ya.shape
    nb = 2 * half_nb
    kb = yb.shape[1]
    row = lambda width: pl.BlockSpec((tm, width), lambda i: (i, 0))
    fixed = lambda r, blk: _resident((r, d), lambda i: (blk, 0))

    def ya_index(i):
        b, tile = i // nb, i % nb
        second = tile >= half_nb
        return (b, jnp.where(second, nb - 1 - tile, tile), second.astype(jnp.int32), 0, 0)

    return pl.pallas_call(
        _outproj_kernel,
        grid=(t // tm,),
        in_specs=[pl.BlockSpec((None, None, None, tm, ka), ya_index),
                  row(kb), fixed(ka, 0), fixed(kb, 1), row(d),
                  fixed(1, 0), fixed(1, 0)],
        out_specs=[row(d), row(d)],
        out_shape=[jax.ShapeDtypeStruct((t, d), F32), jax.ShapeDtypeStruct((t, d), BF16)],
        compiler_params=_params("arbitrary"),
        name="outproj",
    )(ya, yb, w_out, w_out, h, g_post.reshape(1, d), g_next.reshape(1, d))


def _upproj_kernel(x_ref, wg_ref, wu_ref, cwg_ref, cwu_ref, cbg_ref, cbu_ref, o_ref,
                   wgb_ref, wub_ref, ug_ref, uu_ref, *, tm, tiles_per_seq):
    i = pl.program_id(1)

    @pl.when(i == 0)
    def _():
        wgb_ref[...] = wg_ref[...].astype(BF16)
        wub_ref[...] = wu_ref[...].astype(BF16)

    @pl.when(i % tiles_per_seq == 0)
    def _():
        ug_ref[0:8, :] = jnp.zeros((8, ug_ref.shape[1]), F32)
        uu_ref[0:8, :] = jnp.zeros((8, uu_ref.shape[1]), F32)

    x = x_ref[...]
    ug_ref[8:tm + 8, :] = jnp.dot(x, wgb_ref[...], preferred_element_type=F32)
    uu_ref[8:tm + 8, :] = jnp.dot(x, wub_ref[...], preferred_element_type=F32)

    def conv(u_ref, cw_ref, cb_ref):
        acc = cb_ref[...] + cw_ref[2:3, :] * u_ref[8:tm + 8, :]
        acc = acc + cw_ref[1:2, :] * u_ref[7:tm + 7, :]
        return acc + cw_ref[0:1, :] * u_ref[6:tm + 6, :]

    gate = conv(ug_ref, cwg_ref, cbg_ref)
    up = conv(uu_ref, cwu_ref, cbu_ref)
    o_ref[...] = (gate * (1.0 / (1.0 + jnp.exp(-gate))) * up).astype(o_ref.dtype)

    ug_ref[0:8, :] = ug_ref[tm:tm + 8, :]
    uu_ref[0:8, :] = uu_ref[tm:tm + 8, :]


def _upproj(xn, w_up, conv_w, conv_b, layer, seq, tm=1024, tn=512):
    t, k = xn.shape
    nj = D_FF // tn
    lay = lambda r, off: pl.BlockSpec((None, r, tn), lambda j, i: (layer, 0, j + off))
    return pl.pallas_call(
        functools.partial(_upproj_kernel, tm=tm, tiles_per_seq=seq // tm),
        grid=(nj, t // tm),
        in_specs=[pl.BlockSpec((tm, k), lambda j, i: (i, 0)),
                  lay(k, 0), lay(k, nj),
                  lay(CONV_WIDTH, 0), lay(CONV_WIDTH, nj),
                  lay(1, 0), lay(1, nj)],
        out_specs=pl.BlockSpec((tm, tn), lambda j, i: (i, j)),
        out_shape=jax.ShapeDtypeStruct((t, D_FF), BF16),
        scratch_shapes=[pltpu.VMEM((k, tn), BF16), pltpu.VMEM((k, tn), BF16),
                        pltpu.VMEM((tm + 8, tn), F32), pltpu.VMEM((tm + 8, tn), F32)],
        compiler_params=_params("arbitrary", "arbitrary"),
        name="upproj_conv",
    )(xn, w_up, w_up, conv_w, conv_w, conv_b, conv_b)


def _downproj_kernel(a_ref, w_ref, h_ref, gpost_ref, gnext_ref, h_out_ref, hn_out_ref):
    ff = jnp.dot(a_ref[...], w_ref[...], preferred_element_type=F32)
    h = h_ref[...] + _rms(ff, gpost_ref[...])
    h_out_ref[...] = h
    hn_out_ref[...] = _rms(h, gnext_ref[...]).astype(hn_out_ref.dtype)


def _downproj(a, w_down, h, g_post, g_next, tm=256):
    t, d = h.shape
    k = a.shape[1]
    row = lambda width: pl.BlockSpec((tm, width), lambda i: (i, 0))
    fixed = lambda r: _resident((r, d), lambda i: (0, 0))
    return pl.pallas_call(
        _downproj_kernel,
        grid=(t // tm,),
        in_specs=[row(k), fixed(k), row(d), fixed(1), fixed(1)],
        out_specs=[row(d), row(d)],
        out_shape=[jax.ShapeDtypeStruct((t, d), F32), jax.ShapeDtypeStruct((t, d), BF16)],
        compiler_params=_params("arbitrary"),
        name="downproj",
    )(a, w_down, h, g_post.reshape(1, d), g_next.reshape(1, d))


def _ple_kernel(p_ref, hn_ref, wp_ref, wg_ref, h_ref, gpost_ref, gnext_ref,
                h_out_ref, hn_out_ref):
    e = jnp.dot(p_ref[...].astype(BF16), wp_ref[...], preferred_element_type=F32)
    z = jnp.dot(hn_ref[...], wg_ref[...], preferred_element_type=F32)
    gate = 1.0 / (1.0 + jnp.exp(-z))
    h = h_ref[...] + _rms(e * gate, gpost_ref[...])
    h_out_ref[...] = h
    hn_out_ref[...] = _rms(h, gnext_ref[...]).astype(hn_out_ref.dtype)


def _ple(p, hn, w_proj, w_gate, h, g_post, g_next, tm=512):
    t, d = h.shape
    kp = p.shape[1]
    row = lambda width: pl.BlockSpec((tm, width), lambda i: (i, 0))
    fixed = lambda r: _resident((r, d), lambda i: (0, 0))
    return pl.pallas_call(
        _ple_kernel,
        grid=(t // tm,),
        in_specs=[row(kp), row(d), fixed(kp), fixed(d), row(d), fixed(1), fixed(1)],
        out_specs=[row(d), row(d)],
        out_shape=[jax.ShapeDtypeStruct((t, d), F32), jax.ShapeDtypeStruct((t, d), BF16)],
        compiler_params=_params("arbitrary"),
        name="ple",
    )(p, hn, w_proj, w_gate, h, g_post.reshape(1, d), g_next.reshape(1, d))


def kernel(x, p, g_mix_pre, w_in, b_fox_f, w_hgrn_lb, g_hgrn_out, w_out, g_mix_post,
           g_ffn_pre, w_up, conv_w, conv_b, w_down, g_ffn_post,
           g_ple_in, w_ple_gate, w_ple_proj, g_ple_post):
    batch, seq, d = x.shape
    depth = w_in.shape[0]
    t = batch * seq
    off_f = 3 * D_FOX
    off_hg = off_f + HEADS

    qscale = jnp.concatenate([jnp.full((1, D_FOX), LOG2E * HEAD_DIM ** -0.5, F32),
                              jnp.ones((1, 2 * D_FOX), F32)], axis=1)
    ones_hg = jnp.ones((1, 4 * D_HGRN), F32)
    w_hg = w_in[:, :, off_hg:]
    w_f = jnp.pad(w_in[:, :, off_f:off_hg], ((0, 0), (0, 0), (0, 128 - HEADS))).astype(BF16)
    b_f = jnp.pad(b_fox_f, ((0, 0), (0, 128 - HEADS)))
    conv_b3 = conv_b.reshape(depth, 1, -1)

    h = x.reshape(t, d)
    hn = _prenorm(h, g_mix_pre[0])
    for i in range(depth):
        qkv = _proj(hn, w_in, i, 3 * D_FOX, qscale, BF16, name="inproj_qkv")
        zh = _proj(hn, w_hg, i, 4 * D_HGRN, ones_hg, F32, name="inproj_hgrn")
        c3 = _fgate(hn, w_f[i], b_f[i:i + 1], batch, seq)
        y_fox = _attention(qkv.reshape(batch, seq, 3 * D_FOX), c3, batch, seq)
        y_hg = _hgrn(zh, w_hgrn_lb, g_hgrn_out[i], i, batch, seq)

        h, hn = _outproj(y_fox, y_hg, w_out[i].astype(BF16), h,
                         g_mix_post[i], g_ffn_pre[i])
        a = _upproj(hn, w_up, conv_w, conv_b3, i, seq)
        h, hn = _downproj(a, w_down[i].astype(BF16), h, g_ffn_post[i], g_ple_in[i])
        g_next = g_mix_pre[i + 1] if i + 1 < depth else g_mix_pre[i]
        h, hn = _ple(p[i].reshape(t, -1), hn, w_ple_proj[i].astype(BF16),
                     w_ple_gate[i].astype(BF16), h, g_ple_post[i], g_next)
    return h.reshape(batch, seq, d)
```

```python
import functools
import math

import jax
import jax.numpy as jnp
from jax import lax
from jax.experimental import pallas as pl
from jax.experimental.pallas import tpu as pltpu

F32 = jnp.float32
BF16 = jnp.bfloat16

D_MODEL = 2048
D_FOX = 1024
D_HGRN = 1024
HEADS = 8
HEAD_DIM = 128
D_FF = 5632
CONV_WIDTH = 3
EPS = 1e-6
LOG2E = math.log2(math.e)
LANES = 128

HGRN_CHUNK = 64
ATT_BLOCK = 512
UP_SUBTILES = 1
ROW_SUBTILES = 2

VMEM_LIMIT = 56 * 1024 * 1024


def _params(*semantics):
    return pltpu.CompilerParams(dimension_semantics=semantics,
                                vmem_limit_bytes=VMEM_LIMIT)


def _rms(x, g):
    ms = jnp.mean(x * x, axis=-1, keepdims=True)
    return (x * lax.rsqrt(ms + EPS)) * g


def _log_sigmoid(x):
    return jnp.minimum(x, 0.0) - jnp.log(1.0 + jnp.exp(-jnp.abs(x)))


def _resident(shape, index_map):
    return pl.BlockSpec(shape, index_map, pipeline_mode=pl.Buffered(1))


def _prenorm_kernel(x_ref, g_ref, o_ref):
    o_ref[...] = _rms(x_ref[...], g_ref[...]).astype(o_ref.dtype)


def _prenorm(x, g, tm=512):
    t, d = x.shape
    return pl.pallas_call(
        _prenorm_kernel,
        grid=(t // tm,),
        in_specs=[pl.BlockSpec((tm, d), lambda i: (i, 0)),
                  pl.BlockSpec((1, d), lambda i: (0, 0))],
        out_specs=pl.BlockSpec((tm, d), lambda i: (i, 0)),
        out_shape=jax.ShapeDtypeStruct((t, d), BF16),
        compiler_params=_params("arbitrary"),
        name="prenorm",
    )(x, g.reshape(1, d))


def _proj_kernel(x_ref, w_ref, wtail_ref, s_ref, o_ref, wb_ref, *, shift):
    @pl.when(pl.program_id(1) == 0)
    def _():
        if shift:
            w = jnp.concatenate([w_ref[...], wtail_ref[...]], axis=1)
            wb_ref[...] = w[:, shift:shift + wb_ref.shape[1]].astype(BF16)
        else:
            wb_ref[...] = w_ref[...].astype(BF16)

    acc = jnp.dot(x_ref[...], wb_ref[...], preferred_element_type=F32)
    o_ref[...] = (acc * s_ref[...]).astype(o_ref.dtype)


def _proj(x, w, layer, col0, n, colscale, out_dtype, tm=1024, tn=1024, name="proj"):
    t, k = x.shape
    shift = col0 % LANES
    base = col0 - shift
    assert base % tn == 0 and n % tn == 0 and tn % LANES == 0
    tail = lambda j, i: (layer, 0, (base + (j + 1) * tn) // LANES if shift else 0)
    return pl.pallas_call(
        functools.partial(_proj_kernel, shift=shift),
        grid=(n // tn, t // tm),
        in_specs=[pl.BlockSpec((tm, k), lambda j, i: (i, 0)),
                  pl.BlockSpec((None, k, tn), lambda j, i: (layer, 0, base // tn + j)),
                  pl.BlockSpec((None, k, LANES), tail),
                  pl.BlockSpec((1, tn), lambda j, i: (0, j))],
        out_specs=pl.BlockSpec((tm, tn), lambda j, i: (i, j)),
        out_shape=jax.ShapeDtypeStruct((t, n), out_dtype),
        scratch_shapes=[pltpu.VMEM((k, tn), BF16)],
        compiler_params=_params("arbitrary", "arbitrary"),
        name=name,
    )(x, w, w, colscale)


def _fgate_kernel(x_ref, w_ref, b_ref, o_ref, carry_ref, *, tm):
    @pl.when(pl.program_id(1) == 0)
    def _():
        carry_ref[...] = jnp.zeros_like(carry_ref)

    f = jnp.dot(x_ref[...], w_ref[...], preferred_element_type=F32) + b_ref[...]
    c = _log_sigmoid(f)
    row = lax.broadcasted_iota(jnp.int32, c.shape, 0)
    d = 1
    while d < tm:
        c = c + jnp.where(row >= d, pltpu.roll(c, d, 0), 0.0)
        d *= 2
    c = c + carry_ref[0:1, :]
    carry_ref[...] = jnp.broadcast_to(c[tm - 1:tm, :], carry_ref.shape)

    lane = lax.broadcasted_iota(jnp.int32, c.shape, 1)
    x = c * (-LOG2E)
    for hd in range(HEADS):
        col = jnp.broadcast_to(x[:, hd:hd + 1], x.shape)
        hi = col.astype(BF16).astype(F32)
        rest = col - hi
        mid = rest.astype(BF16).astype(F32)
        lo = rest - mid
        parts = jnp.where(lane == 0, hi, jnp.where(lane == 1, mid,
                          jnp.where(lane == 2, lo, 0.0)))
        o_ref[0, hd] = parts.astype(BF16)


def _fgate(xn, w_f, b_f, batch, seq, tm=512):
    t, k = xn.shape
    nt = seq // tm
    return pl.pallas_call(
        functools.partial(_fgate_kernel, tm=tm),
        grid=(batch, nt),
        in_specs=[pl.BlockSpec((tm, k), lambda b, s: (b * nt + s, 0)),
                  pl.BlockSpec((k, 128), lambda b, s: (0, 0)),
                  pl.BlockSpec((1, 128), lambda b, s: (0, 0))],
        out_specs=pl.BlockSpec((1, HEADS, tm, 128), lambda b, s: (b, 0, s, 0)),
        out_shape=jax.ShapeDtypeStruct((batch, HEADS, seq, 128), BF16),
        scratch_shapes=[pltpu.VMEM((8, 128), F32)],
        compiler_params=_params("arbitrary", "arbitrary"),
        name="fgate",
    )(xn, w_f, b_f)


def _attn_kernel(qa_ref, qb_ref, k_ref, v_ref, c_ref, o_ref, vt_ref, q_scr, s_ref,
                 bmax_ref, m_ref, acc_ref, *, blk, nb):
    ip = pl.program_id(2)

    @pl.when(ip == 0)
    def _():
        for hh in range(2):
            for r in range(nb):
                vblk = v_ref[0, r * blk:(r + 1) * blk, hh * HEAD_DIM:(hh + 1) * HEAD_DIM]
                vt_ref[hh * nb + r, 0:HEAD_DIM, :] = vblk.astype(F32).T.astype(BF16)
                vt_ref[hh * nb + r, HEAD_DIM:HEAD_DIM + 8, :] = jnp.ones((8, blk), BF16)

    m_ref[...] = jnp.full(m_ref.shape, -jnp.inf, F32)
    acc_ref[...] = jnp.zeros(acc_ref.shape, F32)

    lane = lax.broadcasted_iota(jnp.int32, (blk, HEAD_DIM), 1)
    ones3 = jnp.where(lane < 3, 1.0, 0.0).astype(BF16)
    for tile, q_ref in enumerate((qa_ref, qb_ref)):
        for hh in range(2):
            q_scr[2 * tile + hh] = jnp.concatenate(
                [q_ref[0, :, hh * HEAD_DIM:(hh + 1) * HEAD_DIM], ones3], axis=1)

    def item(t):
        if t == nb:
            return 1, nb - 1 - ip, True
        if t >= nb // 2:
            return 1, t - ip - 1, False
        in_b = t > ip
        return in_b.astype(jnp.int32), jnp.where(in_b, t - ip - 1, t), t == ip

    def scores(t):
        tile, kb, diag = item(t)
        ks = pl.multiple_of(kb * blk, blk)
        for hh in range(2):
            k_aug = jnp.concatenate(
                [k_ref[0, pl.ds(ks, blk), hh * HEAD_DIM:(hh + 1) * HEAD_DIM],
                 c_ref[0, hh, pl.ds(ks, blk), :]], axis=1)
            s = lax.dot_general(k_aug, q_scr[2 * tile + hh], (((1,), (1,)), ((), ())),
                                preferred_element_type=F32)
            if diag is not False:
                key = lax.broadcasted_iota(jnp.int32, s.shape, 0)
                qry = lax.broadcasted_iota(jnp.int32, s.shape, 1)
                keep = (key <= qry) if diag is True else ((key <= qry) | (t != ip))
                s = jnp.where(keep, s, -jnp.inf)
            bmax_ref[t % 2, hh] = jnp.max(s, axis=0, keepdims=True)
            s_ref[t % 2, hh] = s

    def softmax_pv(t):
        tile, kb, _ = item(t)
        for hh in range(2):
            idx = 2 * tile + hh
            m_old = m_ref[idx]
            m_new = jnp.maximum(m_old, bmax_ref[t % 2, hh])
            alpha = jnp.exp2(m_old - m_new)
            p = jnp.exp2(s_ref[t % 2, hh] - m_new)
            pv = jnp.dot(vt_ref[hh * nb + kb], p.astype(BF16), preferred_element_type=F32)
            acc_ref[idx] = alpha * acc_ref[idx] + pv
            m_ref[idx] = m_new

    scores(0)
    for t in range(nb + 1):
        if t < nb:
            scores(t + 1)
        softmax_pv(t)

    for tile in range(2):
        for hh in range(2):
            acc = acc_ref[2 * tile + hh]
            o = (acc[0:HEAD_DIM] / acc[HEAD_DIM:HEAD_DIM + 1]).T
            o_ref[0, 0, tile, :, hh * HEAD_DIM:(hh + 1) * HEAD_DIM] = o.astype(o_ref.dtype)


def _attention(qkv, c3, batch, seq, blk=ATT_BLOCK):
    nb = seq // blk
    pair = 2 * HEAD_DIM
    npair = HEADS // 2
    return pl.pallas_call(
        functools.partial(_attn_kernel, blk=blk, nb=nb),
        grid=(batch, npair, nb // 2),
        in_specs=[
            pl.BlockSpec((1, blk, pair), lambda b, h, i: (b, i, h)),
            pl.BlockSpec((1, blk, pair), lambda b, h, i: (b, nb - 1 - i, h)),
            pl.BlockSpec((1, seq, pair), lambda b, h, i: (b, 0, npair + h)),
            pl.BlockSpec((1, seq, pair), lambda b, h, i: (b, 0, 2 * npair + h)),
            pl.BlockSpec((1, 2, seq, 128), lambda b, h, i: (b, h, 0, 0)),
        ],
        out_specs=pl.BlockSpec((1, 1, 2, blk, pair), lambda b, h, i: (b, i, 0, 0, h)),
        out_shape=jax.ShapeDtypeStruct((batch, nb // 2, 2, blk, D_FOX), BF16),
        scratch_shapes=[pltpu.VMEM((2 * nb, HEAD_DIM + 8, blk), BF16),
                        pltpu.VMEM((4, blk, pair), BF16),
                        pltpu.VMEM((2, 2, blk, blk), F32),
                        pltpu.VMEM((2, 2, 1, blk), F32),
                        pltpu.VMEM((4, 1, blk), F32),
                        pltpu.VMEM((4, HEAD_DIM + 8, blk), F32)],
        compiler_params=_params("arbitrary", "arbitrary", "arbitrary"),
        name="fox_attention",
    )(qkv, qkv, qkv, qkv, c3)


def _block_middle_rows(cum, chunk):
    row = lax.broadcasted_iota(jnp.int32, cum.shape, 0)
    lanes = cum.shape[-1]
    out = []
    h = chunk // 2
    while h >= 4:
        blocks = cum.reshape(chunk // (2 * h), 2 * h, lanes)
        r = jnp.broadcast_to(blocks[:, h - 1:h, :], blocks.shape).reshape(cum.shape)
        out.append((h, r))
        h //= 2
    tiles = cum.reshape(chunk // 8, 8, lanes)
    pick = lambda j: jnp.broadcast_to(tiles[:, j:j + 1, :], tiles.shape).reshape(cum.shape)
    out.append((2, jnp.where(row % 8 < 4, pick(1), pick(5))))
    out.append((1, jnp.where(row % 2 == 1, pltpu.roll(cum, 1, 0), cum)))
    return out


def _hgrn_kernel(q_ref, f_ref, i_ref, g_ref, wlb_ref, go_ref, o_ref, state_ref,
                 *, chunk, layer):
    @pl.when(pl.program_id(1) == 0)
    def _():
        state_ref[...] = jnp.zeros_like(state_ref)

    if layer > 0:
        w = wlb_ref[...]
        ew = jnp.exp(w - jnp.max(w, axis=0, keepdims=True))
        sm = ew / jnp.sum(ew, axis=0, keepdims=True)
        lb_all = sm[1:2]
        for dpt in range(2, layer + 1):
            lb_all = lb_all + sm[dpt:dpt + 1]
        log_lb_all = jnp.log(lb_all)
        log_1mlb_all = jnp.log1p(-lb_all)

    row = lax.broadcasted_iota(jnp.int32, (chunk, HEAD_DIM), 0)
    rr = lax.broadcasted_iota(jnp.int32, (chunk, chunk), 0)
    cc = lax.broadcasted_iota(jnp.int32, (chunk, chunk), 1)
    pair_mask = {}
    h = chunk // 2
    while h >= 1:
        pair_mask[h] = (((rr // (2 * h)) == (cc // (2 * h)))
                        & (rr % (2 * h) >= h) & (cc % (2 * h) < h))
        h //= 2

    stage = []
    for hd in range(HEADS):
        sl = slice(hd * HEAD_DIM, (hd + 1) * HEAD_DIM)
        q = q_ref[:, sl]
        f = f_ref[:, sl]
        v = i_ref[:, sl]
        if layer == 0:
            log_f = _log_sigmoid(f)
            k = 1.0 / (1.0 + jnp.exp(f))
        else:
            a = log_lb_all[:, sl]
            b = log_1mlb_all[:, sl] + _log_sigmoid(f)
            log_f = jnp.maximum(a, b) + jnp.log(1.0 + jnp.exp(-jnp.abs(a - b)))
            k = (1.0 - lb_all[:, sl]) * (1.0 / (1.0 + jnp.exp(f)))

        cum = log_f * LOG2E
        d = 1
        while d < chunk:
            cum = cum + jnp.where(row >= d, pltpu.roll(cum, d, 0), 0.0)
            d *= 2
        last = cum[chunk - 1:chunk, :]

        sc = jnp.zeros((chunk, chunk), F32)
        for h, r in _block_middle_rows(cum, chunk):
            decay = jnp.exp2(-jnp.abs(cum - r))
            s_l = lax.dot_general((q * decay).astype(BF16), (k * decay).astype(BF16),
                                  (((1,), (1,)), ((), ())), preferred_element_type=F32)
            sc = sc + jnp.where(pair_mask[h], s_l, 0.0)
        stage.append((sc.astype(BF16), v.astype(BF16),
                      jnp.sum(q * k, axis=-1, keepdims=True) * v,
                      (q * jnp.exp2(cum)).astype(BF16),
                      (k * jnp.exp2(last - cum)).astype(BF16), jnp.exp2(last)))

    for hd in range(HEADS):
        sl = slice(hd * HEAD_DIM, (hd + 1) * HEAD_DIM)
        sc16, v16, o_diag, qd, kd, decay_last = stage[hd]
        st = state_ref[hd]
        o = o_diag + jnp.dot(sc16, v16, preferred_element_type=F32)
        o = o + lax.dot_general(qd, st.astype(BF16), (((1,), (1,)), ((), ())),
                                preferred_element_type=F32)
        upd = lax.dot_general(v16, kd, (((0,), (0,)), ((), ())),
                              preferred_element_type=F32)
        state_ref[hd] = st * decay_last + upd

        g = g_ref[:, sl]
        y = _rms(o, go_ref[...]) * (g * (1.0 / (1.0 + jnp.exp(-g))))
        o_ref[:, sl] = y.astype(o_ref.dtype)


def _hgrn(zh, w_lb, g_o, layer, batch, seq, chunk=HGRN_CHUNK):
    t = zh.shape[0]
    nc = seq // chunk
    spec = lambda grp: pl.BlockSpec((chunk, D_HGRN), lambda b, c: (b * nc + c, grp))
    return pl.pallas_call(
        functools.partial(_hgrn_kernel, chunk=chunk, layer=layer),
        grid=(batch, nc),
        in_specs=[spec(0), spec(1), spec(2), spec(3),
                  pl.BlockSpec(w_lb.shape, lambda b, c: (0, 0)),
                  pl.BlockSpec((1, HEAD_DIM), lambda b, c: (0, 0))],
        out_specs=pl.BlockSpec((chunk, D_HGRN), lambda b, c: (b * nc + c, 0)),
        out_shape=jax.ShapeDtypeStruct((t, D_HGRN), BF16),
        scratch_shapes=[pltpu.VMEM((HEADS, HEAD_DIM, HEAD_DIM), F32)],
        compiler_params=_params("arbitrary", "arbitrary"),
        name="hgrn2",
    )(zh, zh, zh, zh, w_lb, g_o.reshape(1, HEAD_DIM))


def _residual_norm_store(branch, rows, h_ref, gpost_ref, gnext_ref, h_out_ref, hn_out_ref):
    h = h_ref[rows, :] + _rms(branch, gpost_ref[...])
    h_out_ref[rows, :] = h
    if hn_out_ref is not None:
        hn_out_ref[rows, :] = _rms(h, gnext_ref[...]).astype(hn_out_ref.dtype)


def _subtile_rows(tm):
    sub = tm // ROW_SUBTILES
    return [slice(s * sub, (s + 1) * sub) for s in range(ROW_SUBTILES)]


def _outproj_kernel(ya_ref, yb_ref, wa_ref, wb_ref, h_ref, gpost_ref, gnext_ref,
                    h_out_ref, hn_out_ref):
    mixes = []
    for rows in _subtile_rows(h_ref.shape[0]):
        mix = jnp.dot(ya_ref[rows, :], wa_ref[...], preferred_element_type=F32)
        mixes.append(mix + jnp.dot(yb_ref[rows, :], wb_ref[...], preferred_element_type=F32))
    for rows, mix in zip(_subtile_rows(h_ref.shape[0]), mixes):
        _residual_norm_store(mix, rows, h_ref, gpost_ref, gnext_ref, h_out_ref, hn_out_ref)


def _outproj(ya, yb, w_out, layer, h, g_post, g_next):
    t, d = h.shape
    _, half_nb, _, tm, ka = ya.shape
    nb = 2 * half_nb
    kb = yb.shape[1]
    row = lambda width: pl.BlockSpec((tm, width), lambda i: (i, 0))
    gain = _resident((1, d), lambda i: (0, 0))

    def ya_index(i):
        b, tile = i // nb, i % nb
        second = tile >= half_nb
        return (b, jnp.where(second, nb - 1 - tile, tile), second.astype(jnp.int32), 0, 0)

    return pl.pallas_call(
        _outproj_kernel,
        grid=(t // tm,),
        in_specs=[pl.BlockSpec((None, None, None, tm, ka), ya_index), row(kb),
                  _resident((None, ka, d), lambda i: (layer, 0, 0)),
                  _resident((None, kb, d), lambda i: (layer, ka // kb, 0)),
                  row(d), gain, gain],
        out_specs=[row(d), row(d)],
        out_shape=[jax.ShapeDtypeStruct((t, d), F32), jax.ShapeDtypeStruct((t, d), BF16)],
        compiler_params=_params("arbitrary"),
        name="outproj",
    )(ya, yb, w_out, w_out, h, g_post.reshape(1, d), g_next.reshape(1, d))


def _upproj_kernel(x_ref, wg_ref, wu_ref, cwg_ref, cwu_ref, cbg_ref, cbu_ref, o_ref,
                   wgb_ref, wub_ref, ug_ref, uu_ref, *, tm, tiles_per_seq):
    i = pl.program_id(1)

    @pl.when(i == 0)
    def _():
        wgb_ref[...] = wg_ref[...].astype(BF16)
        wub_ref[...] = wu_ref[...].astype(BF16)

    @pl.when(i % tiles_per_seq == 0)
    def _():
        ug_ref[0:8, :] = jnp.zeros((8, ug_ref.shape[1]), F32)
        uu_ref[0:8, :] = jnp.zeros((8, uu_ref.shape[1]), F32)

    sub = tm // UP_SUBTILES

    def matmuls(s):
        x = x_ref[s * sub:(s + 1) * sub, :]
        lo = 8 + s * sub
        ug_ref[lo:lo + sub, :] = jnp.dot(x, wgb_ref[...], preferred_element_type=F32)
        uu_ref[lo:lo + sub, :] = jnp.dot(x, wub_ref[...], preferred_element_type=F32)

    def conv(u_ref, cw_ref, cb_ref, lo):
        acc = cb_ref[...] + cw_ref[2:3, :] * u_ref[lo:lo + sub, :]
        acc = acc + cw_ref[1:2, :] * u_ref[lo - 1:lo + sub - 1, :]
        return acc + cw_ref[0:1, :] * u_ref[lo - 2:lo + sub - 2, :]

    def activate(s):
        lo = 8 + s * sub
        gate = conv(ug_ref, cwg_ref, cbg_ref, lo)
        up = conv(uu_ref, cwu_ref, cbu_ref, lo)
        o_ref[s * sub:(s + 1) * sub, :] = (
            gate * (1.0 / (1.0 + jnp.exp(-gate))) * up).astype(o_ref.dtype)

    matmuls(0)
    for s in range(UP_SUBTILES):
        if s + 1 < UP_SUBTILES:
            matmuls(s + 1)
        activate(s)

    ug_ref[0:8, :] = ug_ref[tm:tm + 8, :]
    uu_ref[0:8, :] = uu_ref[tm:tm + 8, :]


def _upproj(xn, w_up, conv_w, conv_b, layer, seq, tm=1024, tn=512):
    t, k = xn.shape
    nj = D_FF // tn
    lay = lambda r, off: pl.BlockSpec((None, r, tn), lambda j, i: (layer, 0, j + off))
    return pl.pallas_call(
        functools.partial(_upproj_kernel, tm=tm, tiles_per_seq=seq // tm),
        grid=(nj, t // tm),
        in_specs=[pl.BlockSpec((tm, k), lambda j, i: (i, 0)),
                  lay(k, 0), lay(k, nj),
                  lay(CONV_WIDTH, 0), lay(CONV_WIDTH, nj),
                  lay(1, 0), lay(1, nj)],
        out_specs=pl.BlockSpec((tm, tn), lambda j, i: (i, j)),
        out_shape=jax.ShapeDtypeStruct((t, D_FF), BF16),
        scratch_shapes=[pltpu.VMEM((k, tn), BF16), pltpu.VMEM((k, tn), BF16),
                        pltpu.VMEM((tm + 8, tn), F32), pltpu.VMEM((tm + 8, tn), F32)],
        compiler_params=_params("arbitrary", "arbitrary"),
        name="upproj_conv",
    )(xn, w_up, w_up, conv_w, conv_w, conv_b, conv_b)


def _downproj_kernel(a_ref, w_ref, h_ref, gpost_ref, gnext_ref, h_out_ref, hn_out_ref):
    ff = jnp.dot(a_ref[...], w_ref[...], preferred_element_type=F32)
    _residual_norm_store(ff, slice(None), h_ref, gpost_ref, gnext_ref, h_out_ref, hn_out_ref)


def _downproj(a, w_down, layer, h, g_post, g_next, tm=256):
    t, d = h.shape
    k = a.shape[1]
    row = lambda width: pl.BlockSpec((tm, width), lambda i: (i, 0))
    gain = _resident((1, d), lambda i: (0, 0))
    return pl.pallas_call(
        _downproj_kernel,
        grid=(t // tm,),
        in_specs=[row(k), _resident((None, k, d), lambda i: (layer, 0, 0)), row(d), gain, gain],
        out_specs=[row(d), row(d)],
        out_shape=[jax.ShapeDtypeStruct((t, d), F32), jax.ShapeDtypeStruct((t, d), BF16)],
        compiler_params=_params("arbitrary"),
        name="downproj",
    )(a, w_down, h, g_post.reshape(1, d), g_next.reshape(1, d))


def _ple_kernel(p_ref, hn_ref, wp_ref, wg_ref, h_ref, gpost_ref, gnext_ref,
                h_out_ref, hn_out_ref=None):
    pairs = []
    for rows in _subtile_rows(h_ref.shape[0]):
        e = jnp.dot(p_ref[rows, :].astype(BF16), wp_ref[...], preferred_element_type=F32)
        z = jnp.dot(hn_ref[rows, :], wg_ref[...], preferred_element_type=F32)
        pairs.append((e, z))
    for rows, (e, z) in zip(_subtile_rows(h_ref.shape[0]), pairs):
        gated = e * (1.0 / (1.0 + jnp.exp(-z)))
        _residual_norm_store(gated, rows, h_ref, gpost_ref, gnext_ref, h_out_ref, hn_out_ref)


def _ple(p, hn, w_proj, w_gate, layer, h, g_post, g_next, tm=512):
    t, d = h.shape
    kp = p.shape[1]
    row = lambda width: pl.BlockSpec((tm, width), lambda i: (i, 0))
    gain = _resident((1, d), lambda i: (0, 0))
    last = g_next is None
    out_shape = [jax.ShapeDtypeStruct((t, d), F32)]
    if not last:
        out_shape.append(jax.ShapeDtypeStruct((t, d), BF16))
    outs = pl.pallas_call(
        _ple_kernel,
        grid=(t // tm,),
        in_specs=[row(kp), row(d),
                  _resident((None, kp, d), lambda i: (layer, 0, 0)),
                  _resident((None, d, d), lambda i: (layer, 0, 0)),
                  row(d), gain, gain],
        out_specs=[row(d)] * len(out_shape),
        out_shape=out_shape,
        compiler_params=_params("arbitrary"),
        name="ple",
    )(p, hn, w_proj, w_gate, h, g_post.reshape(1, d),
      (g_post if last else g_next).reshape(1, d))
    return (outs[0], None) if last else tuple(outs)


def kernel(x, p, g_mix_pre, w_in, b_fox_f, w_hgrn_lb, g_hgrn_out, w_out, g_mix_post,
           g_ffn_pre, w_up, conv_w, conv_b, w_down, g_ffn_post,
           g_ple_in, w_ple_gate, w_ple_proj, g_ple_post):
    batch, seq, d = x.shape
    depth = w_in.shape[0]
    t = batch * seq
    off_f = 3 * D_FOX
    off_hg = off_f + HEADS

    qscale = jnp.concatenate([jnp.full((1, D_FOX), LOG2E * HEAD_DIM ** -0.5, F32),
                              jnp.ones((1, 2 * D_FOX), F32)], axis=1)
    ones_hg = jnp.ones((1, 4 * D_HGRN), F32)
    w_f = jnp.pad(w_in[:, :, off_f:off_hg], ((0, 0), (0, 0), (0, LANES - HEADS))).astype(BF16)
    b_f = jnp.pad(b_fox_f, ((0, 0), (0, LANES - HEADS)))
    conv_b3 = conv_b.reshape(depth, 1, -1)
    w_out16, w_down16 = w_out.astype(BF16), w_down.astype(BF16)
    w_gate16, w_pproj16 = w_ple_gate.astype(BF16), w_ple_proj.astype(BF16)

    h = x.reshape(t, d)
    hn = _prenorm(h, g_mix_pre[0])
    for i in range(depth):
        qkv = _proj(hn, w_in, i, 0, 3 * D_FOX, qscale, BF16, name="inproj_qkv")
        zh = _proj(hn, w_in, i, off_hg, 4 * D_HGRN, ones_hg, F32, name="inproj_hgrn")
        c3 = _fgate(hn, w_f[i], b_f[i:i + 1], batch, seq)
        y_fox = _attention(qkv.reshape(batch, seq, 3 * D_FOX), c3, batch, seq)
        y_hg = _hgrn(zh, w_hgrn_lb, g_hgrn_out[i], i, batch, seq)

        h, hn = _outproj(y_fox, y_hg, w_out16, i, h, g_mix_post[i], g_ffn_pre[i])
        a = _upproj(hn, w_up, conv_w, conv_b3, i, seq)
        h, hn = _downproj(a, w_down16, i, h, g_ffn_post[i], g_ple_in[i])
        g_next = g_mix_pre[i + 1] if i + 1 < depth else None
        h, hn = _ple(p[i].reshape(t, -1), hn, w_pproj16, w_gate16, i, h, g_ple_post[i], g_next)
    return h.reshape(batch, seq, d)
```

```python
import functools
import math

import jax
import jax.numpy as jnp
from jax import lax
from jax.experimental import pallas as pl
from jax.experimental.pallas import tpu as pltpu

F32 = jnp.float32
BF16 = jnp.bfloat16

D_MODEL = 2048
D_FOX = 1024
D_HGRN = 1024
HEADS = 8
HEAD_DIM = 128
D_FF = 5632
CONV_WIDTH = 3
EPS = 1e-6
LOG2E = math.log2(math.e)
LANES = 128
SUBLANES = 8

HGRN_CHUNK = 64
HGRN_CHUNKS_PER_STEP = 4
ATT_BLOCK = 512
ROW_SUBTILES = 2

VMEM_LIMIT = 56 * 1024 * 1024


def _params(*semantics):
    return pltpu.CompilerParams(dimension_semantics=semantics,
                                vmem_limit_bytes=VMEM_LIMIT)


def _rms(x, g):
    ms = jnp.mean(x * x, axis=-1, keepdims=True)
    return (x * lax.rsqrt(ms + EPS)) * g


def _log_sigmoid(x):
    return jnp.minimum(x, 0.0) - jnp.log(1.0 + jnp.exp(-jnp.abs(x)))


def _resident(shape, index_map):
    return pl.BlockSpec(shape, index_map, pipeline_mode=pl.Buffered(1))


def _prenorm_kernel(x_ref, g_ref, o_ref):
    o_ref[...] = _rms(x_ref[...], g_ref[...]).astype(o_ref.dtype)


def _prenorm(x, g, tm=512):
    t, d = x.shape
    return pl.pallas_call(
        _prenorm_kernel,
        grid=(t // tm,),
        in_specs=[pl.BlockSpec((tm, d), lambda i: (i, 0)),
                  pl.BlockSpec((1, d), lambda i: (0, 0))],
        out_specs=pl.BlockSpec((tm, d), lambda i: (i, 0)),
        out_shape=jax.ShapeDtypeStruct((t, d), BF16),
        compiler_params=_params("arbitrary"),
        name="prenorm",
    )(x, g.reshape(1, d))


def _proj_kernel(x_ref, w_ref, wtail_ref, s_ref, o_ref, wb_ref, *, shift):
    @pl.when(pl.program_id(1) == 0)
    def _():
        if shift:
            w = jnp.concatenate([w_ref[...], wtail_ref[...]], axis=0)
            wb_ref[...] = w[shift:shift + wb_ref.shape[0], :].astype(BF16)
        else:
            wb_ref[...] = w_ref[...].astype(BF16)

    acc = lax.dot_general(x_ref[...], wb_ref[...], (((1,), (1,)), ((), ())),
                          preferred_element_type=F32)
    o_ref[...] = (acc * s_ref[...]).astype(o_ref.dtype)


def _proj(x, w_t, layer, row0, n, colscale, out_dtype, tm=1024, tn=1024, name="proj"):
    t, k = x.shape
    shift = row0 % tn
    base = row0 - shift
    assert shift in (0, SUBLANES) and n % tn == 0 and tn % SUBLANES == 0
    tail = lambda j, i: (layer, (base + (j + 1) * tn) // SUBLANES if shift else 0, 0)
    return pl.pallas_call(
        functools.partial(_proj_kernel, shift=shift),
        grid=(n // tn, t // tm),
        in_specs=[pl.BlockSpec((tm, k), lambda j, i: (i, 0)),
                  pl.BlockSpec((None, tn, k), lambda j, i: (layer, base // tn + j, 0)),
                  pl.BlockSpec((None, SUBLANES, k), tail),
                  pl.BlockSpec((1, tn), lambda j, i: (0, j))],
        out_specs=pl.BlockSpec((tm, tn), lambda j, i: (i, j)),
        out_shape=jax.ShapeDtypeStruct((t, n), out_dtype),
        scratch_shapes=[pltpu.VMEM((tn, k), BF16)],
        compiler_params=_params("arbitrary", "arbitrary"),
        name=name,
    )(x, w_t, w_t, colscale)


def _fgate_kernel(x_ref, w_ref, b_ref, o_ref, carry_ref, *, tm):
    @pl.when(pl.program_id(1) == 0)
    def _():
        carry_ref[...] = jnp.zeros_like(carry_ref)

    f = jnp.dot(x_ref[...], w_ref[...], preferred_element_type=F32) + b_ref[...]
    c = _log_sigmoid(f)
    row = lax.broadcasted_iota(jnp.int32, c.shape, 0)
    d = 1
    while d < tm:
        c = c + jnp.where(row >= d, pltpu.roll(c, d, 0), 0.0)
        d *= 2
    c = c + carry_ref[0:1, :]
    carry_ref[...] = jnp.broadcast_to(c[tm - 1:tm, :], carry_ref.shape)

    lane = lax.broadcasted_iota(jnp.int32, c.shape, 1)
    x = c * (-LOG2E)
    for hd in range(HEADS):
        col = jnp.broadcast_to(x[:, hd:hd + 1], x.shape)
        hi = col.astype(BF16).astype(F32)
        rest = col - hi
        mid = rest.astype(BF16).astype(F32)
        lo = rest - mid
        parts = jnp.where(lane == 0, hi, jnp.where(lane == 1, mid,
                          jnp.where(lane == 2, lo, 0.0)))
        o_ref[0, hd] = parts.astype(BF16)


def _fgate(xn, w_f, b_f, batch, seq, tm=512):
    t, k = xn.shape
    nt = seq // tm
    return pl.pallas_call(
        functools.partial(_fgate_kernel, tm=tm),
        grid=(batch, nt),
        in_specs=[pl.BlockSpec((tm, k), lambda b, s: (b * nt + s, 0)),
                  pl.BlockSpec((k, 128), lambda b, s: (0, 0)),
                  pl.BlockSpec((1, 128), lambda b, s: (0, 0))],
        out_specs=pl.BlockSpec((1, HEADS, tm, 128), lambda b, s: (b, 0, s, 0)),
        out_shape=jax.ShapeDtypeStruct((batch, HEADS, seq, 128), BF16),
        scratch_shapes=[pltpu.VMEM((8, 128), F32)],
        compiler_params=_params("arbitrary", "arbitrary"),
        name="fgate",
    )(xn, w_f, b_f)


def _attn_kernel(qa_ref, qb_ref, k_ref, v_ref, c_ref, o_ref, vt_ref, q_scr, s_ref,
                 bmax_ref, m_ref, acc_ref, *, blk, nb):
    ip = pl.program_id(2)

    @pl.when(ip == 0)
    def _():
        for hh in range(2):
            for r in range(nb):
                vblk = v_ref[0, r * blk:(r + 1) * blk, hh * HEAD_DIM:(hh + 1) * HEAD_DIM]
                vt_ref[hh * nb + r, 0:HEAD_DIM, :] = vblk.astype(F32).T.astype(BF16)
                vt_ref[hh * nb + r, HEAD_DIM:HEAD_DIM + 8, :] = jnp.ones((8, blk), BF16)

    m_ref[...] = jnp.full(m_ref.shape, -jnp.inf, F32)
    acc_ref[...] = jnp.zeros(acc_ref.shape, F32)

    lane = lax.broadcasted_iota(jnp.int32, (blk, HEAD_DIM), 1)
    ones3 = jnp.where(lane < 3, 1.0, 0.0).astype(BF16)
    for tile, q_ref in enumerate((qa_ref, qb_ref)):
        for hh in range(2):
            q_scr[2 * tile + hh] = jnp.concatenate(
                [q_ref[0, :, hh * HEAD_DIM:(hh + 1) * HEAD_DIM], ones3], axis=1)

    def item(t):
        if t == nb:
            return 1, nb - 1 - ip, True
        if t >= nb // 2:
            return 1, t - ip - 1, False
        in_b = t > ip
        return in_b.astype(jnp.int32), jnp.where(in_b, t - ip - 1, t), t == ip

    def scores(t):
        tile, kb, diag = item(t)
        ks = pl.multiple_of(kb * blk, blk)
        for hh in range(2):
            k_aug = jnp.concatenate(
                [k_ref[0, pl.ds(ks, blk), hh * HEAD_DIM:(hh + 1) * HEAD_DIM],
                 c_ref[0, hh, pl.ds(ks, blk), :]], axis=1)
            s = lax.dot_general(k_aug, q_scr[2 * tile + hh], (((1,), (1,)), ((), ())),
                                preferred_element_type=F32)
            if diag is not False:
                key = lax.broadcasted_iota(jnp.int32, s.shape, 0)
                qry = lax.broadcasted_iota(jnp.int32, s.shape, 1)
                keep = (key <= qry) if diag is True else ((key <= qry) | (t != ip))
                s = jnp.where(keep, s, -jnp.inf)
            bmax_ref[t % 2, hh] = jnp.max(s, axis=0, keepdims=True)
            s_ref[t % 2, hh] = s

    def softmax_pv(t):
        tile, kb, _ = item(t)
        for hh in range(2):
            idx = 2 * tile + hh
            m_old = m_ref[idx]
            m_new = jnp.maximum(m_old, bmax_ref[t % 2, hh])
            alpha = jnp.exp2(m_old - m_new)
            p = jnp.exp2(s_ref[t % 2, hh] - m_new)
            pv = jnp.dot(vt_ref[hh * nb + kb], p.astype(BF16), preferred_element_type=F32)
            acc_ref[idx] = alpha * acc_ref[idx] + pv
            m_ref[idx] = m_new

    scores(0)
    for t in range(nb + 1):
        if t < nb:
            scores(t + 1)
        softmax_pv(t)

    for tile in range(2):
        for hh in range(2):
            acc = acc_ref[2 * tile + hh]
            o = (acc[0:HEAD_DIM] / acc[HEAD_DIM:HEAD_DIM + 1]).T
            o_ref[0, 0, tile, :, hh * HEAD_DIM:(hh + 1) * HEAD_DIM] = o.astype(o_ref.dtype)


def _attention(qkv, c3, batch, seq, blk=ATT_BLOCK):
    nb = seq // blk
    pair = 2 * HEAD_DIM
    npair = HEADS // 2
    return pl.pallas_call(
        functools.partial(_attn_kernel, blk=blk, nb=nb),
        grid=(batch, npair, nb // 2),
        in_specs=[
            pl.BlockSpec((1, blk, pair), lambda b, h, i: (b, i, h)),
            pl.BlockSpec((1, blk, pair), lambda b, h, i: (b, nb - 1 - i, h)),
            pl.BlockSpec((1, seq, pair), lambda b, h, i: (b, 0, npair + h)),
            pl.BlockSpec((1, seq, pair), lambda b, h, i: (b, 0, 2 * npair + h)),
            pl.BlockSpec((1, 2, seq, 128), lambda b, h, i: (b, h, 0, 0)),
        ],
        out_specs=pl.BlockSpec((1, 1, 2, blk, pair), lambda b, h, i: (b, i, 0, 0, h)),
        out_shape=jax.ShapeDtypeStruct((batch, nb // 2, 2, blk, D_FOX), BF16),
        scratch_shapes=[pltpu.VMEM((2 * nb, HEAD_DIM + 8, blk), BF16),
                        pltpu.VMEM((4, blk, pair), BF16),
                        pltpu.VMEM((2, 2, blk, blk), F32),
                        pltpu.VMEM((2, 2, 1, blk), F32),
                        pltpu.VMEM((4, 1, blk), F32),
                        pltpu.VMEM((4, HEAD_DIM + 8, blk), F32)],
        compiler_params=_params("arbitrary", "arbitrary", "arbitrary"),
        name="fox_attention",
    )(qkv, qkv, qkv, qkv, c3)


def _block_middle_rows(cum, chunk):
    row = lax.broadcasted_iota(jnp.int32, cum.shape, 0)
    lanes = cum.shape[-1]
    out = []
    h = chunk // 2
    while h >= 4:
        blocks = cum.reshape(chunk // (2 * h), 2 * h, lanes)
        r = jnp.broadcast_to(blocks[:, h - 1:h, :], blocks.shape).reshape(cum.shape)
        out.append((h, r))
        h //= 2
    tiles = cum.reshape(chunk // 8, 8, lanes)
    pick = lambda j: jnp.broadcast_to(tiles[:, j:j + 1, :], tiles.shape).reshape(cum.shape)
    out.append((2, jnp.where(row % 8 < 4, pick(1), pick(5))))
    out.append((1, jnp.where(row % 2 == 1, pltpu.roll(cum, 1, 0), cum)))
    return out


def _hgrn_kernel(q_ref, f_ref, i_ref, g_ref, wlb_ref, go_ref, o_ref, state_ref,
                 *, chunk, layer):
    @pl.when(pl.program_id(1) == 0)
    def _():
        state_ref[...] = jnp.zeros_like(state_ref)

    if layer > 0:
        w = wlb_ref[...]
        ew = jnp.exp(w - jnp.max(w, axis=0, keepdims=True))
        sm = ew / jnp.sum(ew, axis=0, keepdims=True)
        lb_all = sm[1:2]
        for dpt in range(2, layer + 1):
            lb_all = lb_all + sm[dpt:dpt + 1]
        log_lb_all = jnp.log(lb_all)
        log_1mlb_all = jnp.log1p(-lb_all)

    row = lax.broadcasted_iota(jnp.int32, (chunk, HEAD_DIM), 0)
    rr = lax.broadcasted_iota(jnp.int32, (chunk, chunk), 0)
    cc = lax.broadcasted_iota(jnp.int32, (chunk, chunk), 1)
    pair_mask = {}
    h = chunk // 2
    while h >= 1:
        pair_mask[h] = (((rr // (2 * h)) == (cc // (2 * h)))
                        & (rr % (2 * h) >= h) & (cc % (2 * h) < h))
        h //= 2

    def scores(rows):
        stage = []
        for hd in range(HEADS):
            sl = slice(hd * HEAD_DIM, (hd + 1) * HEAD_DIM)
            q = q_ref[rows, sl]
            f = f_ref[rows, sl]
            v = i_ref[rows, sl]
            if layer == 0:
                log_f = _log_sigmoid(f)
                k = 1.0 / (1.0 + jnp.exp(f))
            else:
                a = log_lb_all[:, sl]
                b = log_1mlb_all[:, sl] + _log_sigmoid(f)
                log_f = jnp.maximum(a, b) + jnp.log(1.0 + jnp.exp(-jnp.abs(a - b)))
                k = (1.0 - lb_all[:, sl]) * (1.0 / (1.0 + jnp.exp(f)))

            cum = log_f * LOG2E
            d = 1
            while d < chunk:
                cum = cum + jnp.where(row >= d, pltpu.roll(cum, d, 0), 0.0)
                d *= 2
            last = cum[chunk - 1:chunk, :]

            sc = jnp.zeros((chunk, chunk), F32)
            q16, k16 = q.astype(BF16), k.astype(BF16)
            for h, r in _block_middle_rows(cum, chunk):
                decay = jnp.exp2(-jnp.abs(cum - r)).astype(BF16)
                s_l = lax.dot_general(q16 * decay, k16 * decay,
                                      (((1,), (1,)), ((), ())), preferred_element_type=F32)
                sc = sc + jnp.where(pair_mask[h], s_l, 0.0)
            stage.append((sc.astype(BF16), v.astype(BF16),
                          jnp.sum(q * k, axis=-1, keepdims=True) * v,
                          (q * jnp.exp2(cum)).astype(BF16),
                          (k * jnp.exp2(last - cum)).astype(BF16), jnp.exp2(last)))
        return stage

    def recur(rows, stage):
        for hd in range(HEADS):
            sl = slice(hd * HEAD_DIM, (hd + 1) * HEAD_DIM)
            sc16, v16, o_diag, qd, kd, decay_last = stage[hd]
            st = state_ref[hd]
            o = o_diag + jnp.dot(sc16, v16, preferred_element_type=F32)
            o = o + lax.dot_general(qd, st.astype(BF16), (((1,), (1,)), ((), ())),
                                    preferred_element_type=F32)
            upd = lax.dot_general(v16, kd, (((0,), (0,)), ((), ())),
                                  preferred_element_type=F32)
            state_ref[hd] = st * decay_last + upd

            g = g_ref[rows, sl]
            y = _rms(o, go_ref[...]) * (g * (1.0 / (1.0 + jnp.exp(-g))))
            o_ref[rows, sl] = y.astype(o_ref.dtype)

    n_chunks = q_ref.shape[0] // chunk
    rows = [slice(c * chunk, (c + 1) * chunk) for c in range(n_chunks)]
    stage = scores(rows[0])
    for c in range(n_chunks):
        nxt = scores(rows[c + 1]) if c + 1 < n_chunks else None
        recur(rows[c], stage)
        stage = nxt


def _hgrn(zh, w_lb, g_o, layer, batch, seq, chunk=HGRN_CHUNK,
          chunks_per_step=HGRN_CHUNKS_PER_STEP):
    t = zh.shape[0]
    rows = chunk * chunks_per_step
    ns = seq // rows
    spec = lambda grp: pl.BlockSpec((rows, D_HGRN), lambda b, c: (b * ns + c, grp))
    return pl.pallas_call(
        functools.partial(_hgrn_kernel, chunk=chunk, layer=layer),
        grid=(batch, ns),
        in_specs=[spec(0), spec(1), spec(2), spec(3),
                  pl.BlockSpec(w_lb.shape, lambda b, c: (0, 0)),
                  pl.BlockSpec((1, HEAD_DIM), lambda b, c: (0, 0))],
        out_specs=pl.BlockSpec((rows, D_HGRN), lambda b, c: (b * ns + c, 0)),
        out_shape=jax.ShapeDtypeStruct((t, D_HGRN), BF16),
        scratch_shapes=[pltpu.VMEM((HEADS, HEAD_DIM, HEAD_DIM), F32)],
        compiler_params=_params("arbitrary", "arbitrary"),
        name="hgrn2",
    )(zh, zh, zh, zh, w_lb, g_o.reshape(1, HEAD_DIM))


def _residual_norm_store(branch, rows, h_ref, gpost_ref, gnext_ref, h_out_ref, hn_out_ref):
    h = h_ref[rows, :] + _rms(branch, gpost_ref[...])
    h_out_ref[rows, :] = h
    if hn_out_ref is not None:
        hn_out_ref[rows, :] = _rms(h, gnext_ref[...]).astype(hn_out_ref.dtype)


def _subtile_rows(tm):
    sub = tm // ROW_SUBTILES
    return [slice(s * sub, (s + 1) * sub) for s in range(ROW_SUBTILES)]


def _outproj_kernel(ya_ref, yb_ref, wa_ref, wb_ref, h_ref, gpost_ref, gnext_ref,
                    h_out_ref, hn_out_ref):
    mixes = []
    for rows in _subtile_rows(h_ref.shape[0]):
        mix = jnp.dot(ya_ref[rows, :], wa_ref[...], preferred_element_type=F32)
        mixes.append(mix + jnp.dot(yb_ref[rows, :], wb_ref[...], preferred_element_type=F32))
    for rows, mix in zip(_subtile_rows(h_ref.shape[0]), mixes):
        _residual_norm_store(mix, rows, h_ref, gpost_ref, gnext_ref, h_out_ref, hn_out_ref)


def _outproj(ya, yb, w_out, layer, h, g_post, g_next):
    t, d = h.shape
    _, half_nb, _, tm, ka = ya.shape
    nb = 2 * half_nb
    kb = yb.shape[1]
    row = lambda width: pl.BlockSpec((tm, width), lambda i: (i, 0))
    gain = _resident((1, d), lambda i: (0, 0))

    def ya_index(i):
        b, tile = i // nb, i % nb
        second = tile >= half_nb
        return (b, jnp.where(second, nb - 1 - tile, tile), second.astype(jnp.int32), 0, 0)

    return pl.pallas_call(
        _outproj_kernel,
        grid=(t // tm,),
        in_specs=[pl.BlockSpec((None, None, None, tm, ka), ya_index), row(kb),
                  _resident((None, ka, d), lambda i: (layer, 0, 0)),
                  _resident((None, kb, d), lambda i: (layer, ka // kb, 0)),
                  row(d), gain, gain],
        out_specs=[row(d), row(d)],
        out_shape=[jax.ShapeDtypeStruct((t, d), F32), jax.ShapeDtypeStruct((t, d), BF16)],
        compiler_params=_params("arbitrary"),
        name="outproj",
    )(ya, yb, w_out, w_out, h, g_post.reshape(1, d), g_next.reshape(1, d))


def _upproj_kernel(x_ref, wg_ref, wu_ref, cwg_ref, cwu_ref, cbg_ref, cbu_ref, o_ref,
                   wgb_ref, wub_ref, ug_ref, uu_ref, *, tm, tiles_per_seq):
    i = pl.program_id(1)

    @pl.when(i == 0)
    def _():
        wgb_ref[...] = wg_ref[...].astype(BF16)
        wub_ref[...] = wu_ref[...].astype(BF16)

    @pl.when(i % tiles_per_seq == 0)
    def _():
        ug_ref[0:8, :] = jnp.zeros((8, ug_ref.shape[1]), F32)
        uu_ref[0:8, :] = jnp.zeros((8, uu_ref.shape[1]), F32)

    x = x_ref[...]
    ug_ref[8:tm + 8, :] = jnp.dot(x, wgb_ref[...], preferred_element_type=F32)
    uu_ref[8:tm + 8, :] = jnp.dot(x, wub_ref[...], preferred_element_type=F32)

    def conv(u_ref, cw_ref, cb_ref):
        acc = cb_ref[...] + cw_ref[2:3, :] * u_ref[8:tm + 8, :]
        acc = acc + cw_ref[1:2, :] * u_ref[7:tm + 7, :]
        return acc + cw_ref[0:1, :] * u_ref[6:tm + 6, :]

    gate = conv(ug_ref, cwg_ref, cbg_ref)
    up = conv(uu_ref, cwu_ref, cbu_ref)
    o_ref[...] = (gate * (1.0 / (1.0 + jnp.exp(-gate))) * up).astype(o_ref.dtype)

    ug_ref[0:8, :] = ug_ref[tm:tm + 8, :]
    uu_ref[0:8, :] = uu_ref[tm:tm + 8, :]


def _upproj(xn, w_up, conv_w, conv_b, layer, seq, tm=1024, tn=512):
    t, k = xn.shape
    nj = D_FF // tn
    lay = lambda r, off: pl.BlockSpec((None, r, tn), lambda j, i: (layer, 0, j + off))
    return pl.pallas_call(
        functools.partial(_upproj_kernel, tm=tm, tiles_per_seq=seq // tm),
        grid=(nj, t // tm),
        in_specs=[pl.BlockSpec((tm, k), lambda j, i: (i, 0)),
                  lay(k, 0), lay(k, nj),
                  lay(CONV_WIDTH, 0), lay(CONV_WIDTH, nj),
                  lay(1, 0), lay(1, nj)],
        out_specs=pl.BlockSpec((tm, tn), lambda j, i: (i, j)),
        out_shape=jax.ShapeDtypeStruct((t, D_FF), BF16),
        scratch_shapes=[pltpu.VMEM((k, tn), BF16), pltpu.VMEM((k, tn), BF16),
                        pltpu.VMEM((tm + 8, tn), F32), pltpu.VMEM((tm + 8, tn), F32)],
        compiler_params=_params("arbitrary", "arbitrary"),
        name="upproj_conv",
    )(xn, w_up, w_up, conv_w, conv_w, conv_b, conv_b)


def _downproj_kernel(a_ref, w_ref, h_ref, gpost_ref, gnext_ref, h_out_ref, hn_out_ref):
    ff = jnp.dot(a_ref[...], w_ref[...], preferred_element_type=F32)
    _residual_norm_store(ff, slice(None), h_ref, gpost_ref, gnext_ref, h_out_ref, hn_out_ref)


def _downproj(a, w_down, layer, h, g_post, g_next, tm=256):
    t, d = h.shape
    k = a.shape[1]
    row = lambda width: pl.BlockSpec((tm, width), lambda i: (i, 0))
    gain = _resident((1, d), lambda i: (0, 0))
    return pl.pallas_call(
        _downproj_kernel,
        grid=(t // tm,),
        in_specs=[row(k), _resident((None, k, d), lambda i: (layer, 0, 0)), row(d), gain, gain],
        out_specs=[row(d), row(d)],
        out_shape=[jax.ShapeDtypeStruct((t, d), F32), jax.ShapeDtypeStruct((t, d), BF16)],
        compiler_params=_params("arbitrary"),
        name="downproj",
    )(a, w_down, h, g_post.reshape(1, d), g_next.reshape(1, d))


def _ple_kernel(p_ref, hn_ref, wp_ref, wg_ref, h_ref, gpost_ref, gnext_ref,
                h_out_ref, hn_out_ref=None):
    pairs = []
    for rows in _subtile_rows(h_ref.shape[0]):
        e = jnp.dot(p_ref[rows, :].astype(BF16), wp_ref[...], preferred_element_type=F32)
        z = jnp.dot(hn_ref[rows, :], wg_ref[...], preferred_element_type=F32)
        pairs.append((e, z))
    for rows, (e, z) in zip(_subtile_rows(h_ref.shape[0]), pairs):
        gated = e * (1.0 / (1.0 + jnp.exp(-z)))
        _residual_norm_store(gated, rows, h_ref, gpost_ref, gnext_ref, h_out_ref, hn_out_ref)


def _ple(p, hn, w_proj, w_gate, layer, h, g_post, g_next, tm=512):
    t, d = h.shape
    kp = p.shape[1]
    row = lambda width: pl.BlockSpec((tm, width), lambda i: (i, 0))
    gain = _resident((1, d), lambda i: (0, 0))
    last = g_next is None
    out_shape = [jax.ShapeDtypeStruct((t, d), F32)]
    if not last:
        out_shape.append(jax.ShapeDtypeStruct((t, d), BF16))
    outs = pl.pallas_call(
        _ple_kernel,
        grid=(t // tm,),
        in_specs=[row(kp), row(d),
                  _resident((None, kp, d), lambda i: (layer, 0, 0)),
                  _resident((None, d, d), lambda i: (layer, 0, 0)),
                  row(d), gain, gain],
        out_specs=[row(d)] * len(out_shape),
        out_shape=out_shape,
        compiler_params=_params("arbitrary"),
        name="ple",
    )(p, hn, w_proj, w_gate, h, g_post.reshape(1, d),
      (g_post if last else g_next).reshape(1, d))
    return (outs[0], None) if last else tuple(outs)


def kernel(x, p, g_mix_pre, w_in, b_fox_f, w_hgrn_lb, g_hgrn_out, w_out, g_mix_post,
           g_ffn_pre, w_up, conv_w, conv_b, w_down, g_ffn_post,
           g_ple_in, w_ple_gate, w_ple_proj, g_ple_post):
    batch, seq, d = x.shape
    depth = w_in.shape[0]
    t = batch * seq
    off_f = 3 * D_FOX
    off_hg = off_f + HEADS

    qscale = jnp.concatenate([jnp.full((1, D_FOX), LOG2E * HEAD_DIM ** -0.5, F32),
                              jnp.ones((1, 2 * D_FOX), F32)], axis=1)
    ones_hg = jnp.ones((1, 4 * D_HGRN), F32)
    w_in_t = jnp.swapaxes(w_in, 1, 2)
    w_f = jnp.pad(w_in[:, :, off_f:off_hg], ((0, 0), (0, 0), (0, LANES - HEADS))).astype(BF16)
    b_f = jnp.pad(b_fox_f, ((0, 0), (0, LANES - HEADS)))
    conv_b3 = conv_b.reshape(depth, 1, -1)
    w_out16, w_down16 = w_out.astype(BF16), w_down.astype(BF16)
    w_gate16, w_pproj16 = w_ple_gate.astype(BF16), w_ple_proj.astype(BF16)

    h = x.reshape(t, d)
    hn = _prenorm(h, g_mix_pre[0])
    for i in range(depth):
        qkv = _proj(hn, w_in_t, i, 0, 3 * D_FOX, qscale, BF16, name="inproj_qkv")
        zh = _proj(hn, w_in_t, i, off_hg, 4 * D_HGRN, ones_hg, F32, name="inproj_hgrn")
        c3 = _fgate(hn, w_f[i], b_f[i:i + 1], batch, seq)
        y_fox = _attention(qkv.reshape(batch, seq, 3 * D_FOX), c3, batch, seq)
        y_hg = _hgrn(zh, w_hgrn_lb, g_hgrn_out[i], i, batch, seq)

        h, hn = _outproj(y_fox, y_hg, w_out16, i, h, g_mix_post[i], g_ffn_pre[i])
        a = _upproj(hn, w_up, conv_w, conv_b3, i, seq)
        h, hn = _downproj(a, w_down16, i, h, g_ffn_post[i], g_ple_in[i])
        g_next = g_mix_pre[i + 1] if i + 1 < depth else None
        h, hn = _ple(p[i].reshape(t, -1), hn, w_pproj16, w_gate16, i, h, g_ple_post[i], g_next)
    return h.reshape(batch, seq, d)
```

```python
import functools
import math

import jax
import jax.numpy as jnp
from jax import lax
from jax.experimental import pallas as pl
from jax.experimental.pallas import tpu as pltpu

F32 = jnp.float32
BF16 = jnp.bfloat16

D_MODEL = 2048
D_FOX = 1024
D_HGRN = 1024
HEADS = 8
HEAD_DIM = 128
D_FF = 5632
CONV_WIDTH = 3
EPS = 1e-6
LOG2E = math.log2(math.e)
LANES = 128
SUBLANES = 8

HGRN_CHUNK = 64
HGRN_CHUNKS_PER_STEP = 4
ATT_BLOCK = 512
ATT_GROUP = 4
ROW_SUBTILES = 2

VMEM_LIMIT = 56 * 1024 * 1024


def _params(*semantics):
    return pltpu.CompilerParams(dimension_semantics=semantics,
                                vmem_limit_bytes=VMEM_LIMIT)


def _rms(x, g):
    ms = jnp.mean(x * x, axis=-1, keepdims=True)
    return (x * lax.rsqrt(ms + EPS)) * g


def _log_sigmoid(x):
    return jnp.minimum(x, 0.0) - jnp.log(1.0 + jnp.exp(-jnp.abs(x)))


def _resident(shape, index_map):
    return pl.BlockSpec(shape, index_map, pipeline_mode=pl.Buffered(1))


def _prenorm_kernel(x_ref, g_ref, o_ref):
    o_ref[...] = _rms(x_ref[...], g_ref[...]).astype(o_ref.dtype)


def _prenorm(x, g, tm=512):
    t, d = x.shape
    return pl.pallas_call(
        _prenorm_kernel,
        grid=(t // tm,),
        in_specs=[pl.BlockSpec((tm, d), lambda i: (i, 0)),
                  pl.BlockSpec((1, d), lambda i: (0, 0))],
        out_specs=pl.BlockSpec((tm, d), lambda i: (i, 0)),
        out_shape=jax.ShapeDtypeStruct((t, d), BF16),
        compiler_params=_params("arbitrary"),
        name="prenorm",
    )(x, g.reshape(1, d))


def _proj_kernel(x_ref, w_ref, wtail_ref, s_ref, o_ref, wb_ref, *, shift):
    @pl.when(pl.program_id(1) == 0)
    def _():
        if shift:
            w = jnp.concatenate([w_ref[...], wtail_ref[...]], axis=0)
            wb_ref[...] = w[shift:shift + wb_ref.shape[0], :].astype(BF16)
        else:
            wb_ref[...] = w_ref[...].astype(BF16)

    acc = lax.dot_general(x_ref[...], wb_ref[...], (((1,), (1,)), ((), ())),
                          preferred_element_type=F32)
    o_ref[...] = (acc * s_ref[...]).astype(o_ref.dtype)


def _proj(x, w_t, layer, row0, n, colscale, out_dtype, tm=1024, tn=1024, name="proj"):
    t, k = x.shape
    shift = row0 % tn
    base = row0 - shift
    assert shift in (0, SUBLANES) and n % tn == 0 and tn % SUBLANES == 0
    tail = lambda j, i: (layer, (base + (j + 1) * tn) // SUBLANES if shift else 0, 0)
    return pl.pallas_call(
        functools.partial(_proj_kernel, shift=shift),
        grid=(n // tn, t // tm),
        in_specs=[pl.BlockSpec((tm, k), lambda j, i: (i, 0)),
                  pl.BlockSpec((None, tn, k), lambda j, i: (layer, base // tn + j, 0)),
                  pl.BlockSpec((None, SUBLANES, k), tail),
                  pl.BlockSpec((1, tn), lambda j, i: (0, j))],
        out_specs=pl.BlockSpec((tm, tn), lambda j, i: (i, j)),
        out_shape=jax.ShapeDtypeStruct((t, n), out_dtype),
        scratch_shapes=[pltpu.VMEM((tn, k), BF16)],
        compiler_params=_params("arbitrary", "arbitrary"),
        name=name,
    )(x, w_t, w_t, colscale)


def _fgate_kernel(x_ref, w_ref, b_ref, o_ref, carry_ref, *, tm):
    @pl.when(pl.program_id(1) == 0)
    def _():
        carry_ref[...] = jnp.zeros_like(carry_ref)

    f = jnp.dot(x_ref[...], w_ref[...], preferred_element_type=F32) + b_ref[...]
    c = _log_sigmoid(f)
    row = lax.broadcasted_iota(jnp.int32, c.shape, 0)
    d = 1
    while d < tm:
        c = c + jnp.where(row >= d, pltpu.roll(c, d, 0), 0.0)
        d *= 2
    c = c + carry_ref[0:1, :]
    carry_ref[...] = jnp.broadcast_to(c[tm - 1:tm, :], carry_ref.shape)

    x = c * (-LOG2E)
    hi = x.astype(BF16).astype(F32)
    rest = x - hi
    mid = rest.astype(BF16).astype(F32)
    lo = rest - mid
    lane = lax.broadcasted_iota(jnp.int32, c.shape, 1)
    parts = jnp.where(lane < HEADS, hi,
                      jnp.where(lane < 2 * HEADS, pltpu.roll(mid, HEADS, 1),
                                jnp.where(lane < 3 * HEADS, pltpu.roll(lo, 2 * HEADS, 1), 0.0)))
    o_ref[0] = parts.astype(BF16)


def _fgate(xn, w_f, b_f, batch, seq, tm=512):
    t, k = xn.shape
    nt = seq // tm
    return pl.pallas_call(
        functools.partial(_fgate_kernel, tm=tm),
        grid=(batch, nt),
        in_specs=[pl.BlockSpec((tm, k), lambda b, s: (b * nt + s, 0)),
                  pl.BlockSpec((k, LANES), lambda b, s: (0, 0)),
                  pl.BlockSpec((1, LANES), lambda b, s: (0, 0))],
        out_specs=pl.BlockSpec((1, tm, LANES), lambda b, s: (b, s, 0)),
        out_shape=jax.ShapeDtypeStruct((batch, seq, LANES), BF16),
        scratch_shapes=[pltpu.VMEM((8, LANES), F32)],
        compiler_params=_params("arbitrary", "arbitrary"),
        name="fgate",
    )(xn, w_f, b_f)


def _attn_kernel(qa_ref, qb_ref, k_ref, v_ref, c_ref, o_ref, vt_ref, q_scr, m_ref, acc_ref,
                 *, blk, nb, group):
    hg, ip = pl.program_id(1), pl.program_id(2)
    head_cols = [slice(hh * HEAD_DIM, (hh + 1) * HEAD_DIM) for hh in range(group)]

    @pl.when(ip == 0)
    def _():
        for hh in range(group):
            for r in range(nb):
                vblk = v_ref[0, r * blk:(r + 1) * blk, head_cols[hh]]
                vt_ref[hh * nb + r, 0:HEAD_DIM, :] = vblk.astype(F32).T.astype(BF16)
                vt_ref[hh * nb + r, HEAD_DIM:HEAD_DIM + 8, :] = jnp.ones((8, blk), BF16)

    m_ref[...] = jnp.full(m_ref.shape, -jnp.inf, F32)
    acc_ref[...] = jnp.zeros(acc_ref.shape, F32)

    lane = lax.broadcasted_iota(jnp.int32, (blk, HEAD_DIM), 1)
    for hh in range(group):
        head = hg * group + hh
        gate_lanes = (lane == head) | (lane == head + HEADS) | (lane == head + 2 * HEADS)
        ones = jnp.where(gate_lanes, 1.0, 0.0).astype(BF16)
        for tile, q_ref in enumerate((qa_ref, qb_ref)):
            q_scr[group * tile + hh] = jnp.concatenate([q_ref[0, :, head_cols[hh]], ones], axis=1)

    def item(t):
        if t == nb:
            return 1, nb - 1 - ip, True
        if t >= nb // 2:
            return 1, t - ip - 1, False
        in_b = t > ip
        return in_b.astype(jnp.int32), jnp.where(in_b, t - ip - 1, t), t == ip

    def scores(t):
        tile, kb, diag = item(t)
        ks = pl.multiple_of(kb * blk, blk)
        gate = c_ref[0, pl.ds(ks, blk), :]
        out = []
        for hh in range(group):
            k_aug = jnp.concatenate([k_ref[0, pl.ds(ks, blk), head_cols[hh]], gate], axis=1)
            s = lax.dot_general(k_aug, q_scr[group * tile + hh], (((1,), (1,)), ((), ())),
                                preferred_element_type=F32)
            if diag is not False:
                key = lax.broadcasted_iota(jnp.int32, s.shape, 0)
                qry = lax.broadcasted_iota(jnp.int32, s.shape, 1)
                keep = (key <= qry) if diag is True else ((key <= qry) | (t != ip))
                s = jnp.where(keep, s, -jnp.inf)
            out.append((s, jnp.max(s, axis=0, keepdims=True)))
        return out

    def softmax_pv(t, scored):
        tile, kb, _ = item(t)
        for hh in range(group):
            idx = group * tile + hh
            s, bmax = scored[hh]
            m_old = m_ref[idx]
            m_new = jnp.maximum(m_old, bmax)
            alpha = jnp.exp2(m_old - m_new)
            p = jnp.exp2(s - m_new)
            pv = jnp.dot(vt_ref[hh * nb + kb], p.astype(BF16), preferred_element_type=F32)
            acc_ref[idx] = alpha * acc_ref[idx] + pv
            m_ref[idx] = m_new

    scored = scores(0)
    for t in range(nb + 1):
        ahead = scores(t + 1) if t < nb else None
        softmax_pv(t, scored)
        scored = ahead

    for tile in range(2):
        for hh in range(group):
            acc = acc_ref[group * tile + hh]
            o = (acc[0:HEAD_DIM] / acc[HEAD_DIM:HEAD_DIM + 1]).T
            o_ref[0, 0, tile, :, head_cols[hh]] = o.astype(o_ref.dtype)


def _attention(qkv, gate, batch, seq, blk=ATT_BLOCK, group=ATT_GROUP):
    nb = seq // blk
    width = group * HEAD_DIM
    ngroup = HEADS // group
    return pl.pallas_call(
        functools.partial(_attn_kernel, blk=blk, nb=nb, group=group),
        grid=(batch, ngroup, nb // 2),
        in_specs=[
            pl.BlockSpec((1, blk, width), lambda b, h, i: (b, i, h)),
            pl.BlockSpec((1, blk, width), lambda b, h, i: (b, nb - 1 - i, h)),
            pl.BlockSpec((1, seq, width), lambda b, h, i: (b, 0, ngroup + h)),
            pl.BlockSpec((1, seq, width), lambda b, h, i: (b, 0, 2 * ngroup + h)),
            pl.BlockSpec((1, seq, LANES), lambda b, h, i: (b, 0, 0)),
        ],
        out_specs=pl.BlockSpec((1, 1, 2, blk, width), lambda b, h, i: (b, i, 0, 0, h)),
        out_shape=jax.ShapeDtypeStruct((batch, nb // 2, 2, blk, D_FOX), BF16),
        scratch_shapes=[pltpu.VMEM((group * nb, HEAD_DIM + 8, blk), BF16),
                        pltpu.VMEM((2 * group, blk, 2 * HEAD_DIM), BF16),
                        pltpu.VMEM((2 * group, 1, blk), F32),
                        pltpu.VMEM((2 * group, HEAD_DIM + 8, blk), F32)],
        compiler_params=_params("arbitrary", "arbitrary", "arbitrary"),
        name="fox_attention",
    )(qkv, qkv, qkv, qkv, gate)


def _block_middle_rows(cum, chunk):
    row = lax.broadcasted_iota(jnp.int32, cum.shape, 0)
    lanes = cum.shape[-1]
    out = []
    h = chunk // 2
    while h >= 4:
        blocks = cum.reshape(chunk // (2 * h), 2 * h, lanes)
        r = jnp.broadcast_to(blocks[:, h - 1:h, :], blocks.shape).reshape(cum.shape)
        out.append((h, r))
        h //= 2
    tiles = cum.reshape(chunk // 8, 8, lanes)
    pick = lambda j: jnp.broadcast_to(tiles[:, j:j + 1, :], tiles.shape).reshape(cum.shape)
    out.append((2, jnp.where(row % 8 < 4, pick(1), pick(5))))
    out.append((1, jnp.where(row % 2 == 1, pltpu.roll(cum, 1, 0), cum)))
    return out


def _hgrn_kernel(q_ref, f_ref, i_ref, g_ref, wlb_ref, go_ref, o_ref, state_ref,
                 *, chunk, layer):
    @pl.when(pl.program_id(1) == 0)
    def _():
        state_ref[...] = jnp.zeros_like(state_ref)

    if layer > 0:
        w = wlb_ref[...]
        ew = jnp.exp(w - jnp.max(w, axis=0, keepdims=True))
        sm = ew / jnp.sum(ew, axis=0, keepdims=True)
        lb_all = sm[1:2]
        for dpt in range(2, layer + 1):
            lb_all = lb_all + sm[dpt:dpt + 1]
        log_lb_all = jnp.log(lb_all)
        log_1mlb_all = jnp.log1p(-lb_all)

    row = lax.broadcasted_iota(jnp.int32, (chunk, HEAD_DIM), 0)
    rr = lax.broadcasted_iota(jnp.int32, (chunk, chunk), 0)
    cc = lax.broadcasted_iota(jnp.int32, (chunk, chunk), 1)
    pair_mask = {}
    h = chunk // 2
    while h >= 1:
        pair_mask[h] = (((rr // (2 * h)) == (cc // (2 * h)))
                        & (rr % (2 * h) >= h) & (cc % (2 * h) < h))
        h //= 2

    def scores(rows):
        stage = []
        for hd in range(HEADS):
            sl = slice(hd * HEAD_DIM, (hd + 1) * HEAD_DIM)
            q = q_ref[rows, sl]
            f = f_ref[rows, sl]
            v = i_ref[rows, sl]
            if layer == 0:
                log_f = _log_sigmoid(f)
                k = 1.0 / (1.0 + jnp.exp(f))
            else:
                a = log_lb_all[:, sl]
                b = log_1mlb_all[:, sl] + _log_sigmoid(f)
                log_f = jnp.maximum(a, b) + jnp.log(1.0 + jnp.exp(-jnp.abs(a - b)))
                k = (1.0 - lb_all[:, sl]) * (1.0 / (1.0 + jnp.exp(f)))

            cum = log_f * LOG2E
            d = 1
            while d < chunk:
                cum = cum + jnp.where(row >= d, pltpu.roll(cum, d, 0), 0.0)
                d *= 2
            last = cum[chunk - 1:chunk, :]

            sc = jnp.zeros((chunk, chunk), F32)
            q16, k16 = q.astype(BF16), k.astype(BF16)
            for h, r in _block_middle_rows(cum, chunk):
                decay = jnp.exp2(-jnp.abs(cum - r)).astype(BF16)
                s_l = lax.dot_general(q16 * decay, k16 * decay,
                                      (((1,), (1,)), ((), ())), preferred_element_type=F32)
                sc = sc + jnp.where(pair_mask[h], s_l, 0.0)
            stage.append((sc.astype(BF16), v.astype(BF16),
                          jnp.sum(q * k, axis=-1, keepdims=True) * v,
                          (q * jnp.exp2(cum)).astype(BF16),
                          (k * jnp.exp2(last - cum)).astype(BF16), jnp.exp2(last)))
        return stage

    def recur(rows, stage):
        for hd in range(HEADS):
            sl = slice(hd * HEAD_DIM, (hd + 1) * HEAD_DIM)
            sc16, v16, o_diag, qd, kd, decay_last = stage[hd]
            st = state_ref[hd]
            o = o_diag + jnp.dot(sc16, v16, preferred_element_type=F32)
            o = o + lax.dot_general(qd, st.astype(BF16), (((1,), (1,)), ((), ())),
                                    preferred_element_type=F32)
            upd = lax.dot_general(v16, kd, (((0,), (0,)), ((), ())),
                                  preferred_element_type=F32)
            state_ref[hd] = st * decay_last + upd

            g = g_ref[rows, sl]
            y = _rms(o, go_ref[...]) * (g * (1.0 / (1.0 + jnp.exp(-g))))
            o_ref[rows, sl] = y.astype(o_ref.dtype)

    n_chunks = q_ref.shape[0] // chunk
    rows = [slice(c * chunk, (c + 1) * chunk) for c in range(n_chunks)]
    stage = scores(rows[0])
    for c in range(n_chunks):
        nxt = scores(rows[c + 1]) if c + 1 < n_chunks else None
        recur(rows[c], stage)
        stage = nxt


def _hgrn(zh, w_lb, g_o, layer, batch, seq, chunk=HGRN_CHUNK,
          chunks_per_step=HGRN_CHUNKS_PER_STEP):
    t = zh.shape[0]
    rows = chunk * chunks_per_step
    ns = seq // rows
    spec = lambda grp: pl.BlockSpec((rows, D_HGRN), lambda b, c: (b * ns + c, grp))
    return pl.pallas_call(
        functools.partial(_hgrn_kernel, chunk=chunk, layer=layer),
        grid=(batch, ns),
        in_specs=[spec(0), spec(1), spec(2), spec(3),
                  pl.BlockSpec(w_lb.shape, lambda b, c: (0, 0)),
                  pl.BlockSpec((1, HEAD_DIM), lambda b, c: (0, 0))],
        out_specs=pl.BlockSpec((rows, D_HGRN), lambda b, c: (b * ns + c, 0)),
        out_shape=jax.ShapeDtypeStruct((t, D_HGRN), BF16),
        scratch_shapes=[pltpu.VMEM((HEADS, HEAD_DIM, HEAD_DIM), F32)],
        compiler_params=_params("arbitrary", "arbitrary"),
        name="hgrn2",
    )(zh, zh, zh, zh, w_lb, g_o.reshape(1, HEAD_DIM))


def _residual_norm_store(branch, rows, h_ref, gpost_ref, gnext_ref, h_out_ref, hn_out_ref):
    h = h_ref[rows, :] + _rms(branch, gpost_ref[...])
    h_out_ref[rows, :] = h
    if hn_out_ref is not None:
        hn_out_ref[rows, :] = _rms(h, gnext_ref[...]).astype(hn_out_ref.dtype)


def _subtile_rows(tm):
    sub = tm // ROW_SUBTILES
    return [slice(s * sub, (s + 1) * sub) for s in range(ROW_SUBTILES)]


def _outproj_kernel(ya_ref, yb_ref, wa_ref, wb_ref, h_ref, gpost_ref, gnext_ref,
                    h_out_ref, hn_out_ref):
    mixes = []
    for rows in _subtile_rows(h_ref.shape[0]):
        mix = jnp.dot(ya_ref[rows, :], wa_ref[...], preferred_element_type=F32)
        mixes.append(mix + jnp.dot(yb_ref[rows, :], wb_ref[...], preferred_element_type=F32))
    for rows, mix in zip(_subtile_rows(h_ref.shape[0]), mixes):
        _residual_norm_store(mix, rows, h_ref, gpost_ref, gnext_ref, h_out_ref, hn_out_ref)


def _outproj(ya, yb, w_out, layer, h, g_post, g_next):
    t, d = h.shape
    _, half_nb, _, tm, ka = ya.shape
    nb = 2 * half_nb
    kb = yb.shape[1]
    row = lambda width: pl.BlockSpec((tm, width), lambda i: (i, 0))
    gain = _resident((1, d), lambda i: (0, 0))

    def ya_index(i):
        b, tile = i // nb, i % nb
        second = tile >= half_nb
        return (b, jnp.where(second, nb - 1 - tile, tile), second.astype(jnp.int32), 0, 0)

    return pl.pallas_call(
        _outproj_kernel,
        grid=(t // tm,),
        in_specs=[pl.BlockSpec((None, None, None, tm, ka), ya_index), row(kb),
                  _resident((None, ka, d), lambda i: (layer, 0, 0)),
                  _resident((None, kb, d), lambda i: (layer, ka // kb, 0)),
                  row(d), gain, gain],
        out_specs=[row(d), row(d)],
        out_shape=[jax.ShapeDtypeStruct((t, d), F32), jax.ShapeDtypeStruct((t, d), BF16)],
        compiler_params=_params("arbitrary"),
        name="outproj",
    )(ya, yb, w_out, w_out, h, g_post.reshape(1, d), g_next.reshape(1, d))


def _upproj_kernel(x_ref, wg_ref, wu_ref, cwg_ref, cwu_ref, cbg_ref, cbu_ref, o_ref,
                   wgb_ref, wub_ref, ug_ref, uu_ref, *, tm, tiles_per_seq):
    i = pl.program_id(1)

    @pl.when(i == 0)
    def _():
        wgb_ref[...] = wg_ref[...].astype(BF16)
        wub_ref[...] = wu_ref[...].astype(BF16)

    @pl.when(i % tiles_per_seq == 0)
    def _():
        ug_ref[0:8, :] = jnp.zeros((8, ug_ref.shape[1]), F32)
        uu_ref[0:8, :] = jnp.zeros((8, uu_ref.shape[1]), F32)

    x = x_ref[...]
    ug_ref[8:tm + 8, :] = jnp.dot(x, wgb_ref[...], preferred_element_type=F32)
    uu_ref[8:tm + 8, :] = jnp.dot(x, wub_ref[...], preferred_element_type=F32)

    def conv(u_ref, cw_ref, cb_ref):
        acc = cb_ref[...] + cw_ref[2:3, :] * u_ref[8:tm + 8, :]
        acc = acc + cw_ref[1:2, :] * u_ref[7:tm + 7, :]
        return acc + cw_ref[0:1, :] * u_ref[6:tm + 6, :]

    gate = conv(ug_ref, cwg_ref, cbg_ref)
    up = conv(uu_ref, cwu_ref, cbu_ref)
    o_ref[...] = (gate * (1.0 / (1.0 + jnp.exp(-gate))) * up).astype(o_ref.dtype)

    ug_ref[0:8, :] = ug_ref[tm:tm + 8, :]
    uu_ref[0:8, :] = uu_ref[tm:tm + 8, :]


def _upproj(xn, w_up, conv_w, conv_b, layer, seq, tm=1024, tn=512):
    t, k = xn.shape
    nj = D_FF // tn
    lay = lambda r, off: pl.BlockSpec((None, r, tn), lambda j, i: (layer, 0, j + off))
    return pl.pallas_call(
        functools.partial(_upproj_kernel, tm=tm, tiles_per_seq=seq // tm),
        grid=(nj, t // tm),
        in_specs=[pl.BlockSpec((tm, k), lambda j, i: (i, 0)),
                  lay(k, 0), lay(k, nj),
                  lay(CONV_WIDTH, 0), lay(CONV_WIDTH, nj),
                  lay(1, 0), lay(1, nj)],
        out_specs=pl.BlockSpec((tm, tn), lambda j, i: (i, j)),
        out_shape=jax.ShapeDtypeStruct((t, D_FF), BF16),
        scratch_shapes=[pltpu.VMEM((k, tn), BF16), pltpu.VMEM((k, tn), BF16),
                        pltpu.VMEM((tm + 8, tn), F32), pltpu.VMEM((tm + 8, tn), F32)],
        compiler_params=_params("arbitrary", "arbitrary"),
        name="upproj_conv",
    )(xn, w_up, w_up, conv_w, conv_w, conv_b, conv_b)


def _downproj_kernel(a_ref, w_ref, h_ref, gpost_ref, gnext_ref, h_out_ref, hn_out_ref):
    ff = jnp.dot(a_ref[...], w_ref[...], preferred_element_type=F32)
    _residual_norm_store(ff, slice(None), h_ref, gpost_ref, gnext_ref, h_out_ref, hn_out_ref)


def _downproj(a, w_down, layer, h, g_post, g_next, tm=256):
    t, d = h.shape
    k = a.shape[1]
    row = lambda width: pl.BlockSpec((tm, width), lambda i: (i, 0))
    gain = _resident((1, d), lambda i: (0, 0))
    return pl.pallas_call(
        _downproj_kernel,
        grid=(t // tm,),
        in_specs=[row(k), _resident((None, k, d), lambda i: (layer, 0, 0)), row(d), gain, gain],
        out_specs=[row(d), row(d)],
        out_shape=[jax.ShapeDtypeStruct((t, d), F32), jax.ShapeDtypeStruct((t, d), BF16)],
        compiler_params=_params("arbitrary"),
        name="downproj",
    )(a, w_down, h, g_post.reshape(1, d), g_next.reshape(1, d))


def _ple_kernel(p_ref, hn_ref, wp_ref, wg_ref, h_ref, gpost_ref, gnext_ref,
                h_out_ref, hn_out_ref=None):
    pairs = []
    for rows in _subtile_rows(h_ref.shape[0]):
        e = jnp.dot(p_ref[rows, :].astype(BF16), wp_ref[...], preferred_element_type=F32)
        z = jnp.dot(hn_ref[rows, :], wg_ref[...], preferred_element_type=F32)
        pairs.append((e, z))
    for rows, (e, z) in zip(_subtile_rows(h_ref.shape[0]), pairs):
        gated = e * (1.0 / (1.0 + jnp.exp(-z)))
        _residual_norm_store(gated, rows, h_ref, gpost_ref, gnext_ref, h_out_ref, hn_out_ref)


def _ple(p, hn, w_proj, w_gate, layer, h, g_post, g_next, tm=512):
    t, d = h.shape
    kp = p.shape[1]
    row = lambda width: pl.BlockSpec((tm, width), lambda i: (i, 0))
    gain = _resident((1, d), lambda i: (0, 0))
    last = g_next is None
    out_shape = [jax.ShapeDtypeStruct((t, d), F32)]
    if not last:
        out_shape.append(jax.ShapeDtypeStruct((t, d), BF16))
    outs = pl.pallas_call(
        _ple_kernel,
        grid=(t // tm,),
        in_specs=[row(kp), row(d),
                  _resident((None, kp, d), lambda i: (layer, 0, 0)),
                  _resident((None, d, d), lambda i: (layer, 0, 0)),
                  row(d), gain, gain],
        out_specs=[row(d)] * len(out_shape),
        out_shape=out_shape,
        compiler_params=_params("arbitrary"),
        name="ple",
    )(p, hn, w_proj, w_gate, h, g_post.reshape(1, d),
      (g_post if last else g_next).reshape(1, d))
    return (outs[0], None) if last else tuple(outs)


def kernel(x, p, g_mix_pre, w_in, b_fox_f, w_hgrn_lb, g_hgrn_out, w_out, g_mix_post,
           g_ffn_pre, w_up, conv_w, conv_b, w_down, g_ffn_post,
           g_ple_in, w_ple_gate, w_ple_proj, g_ple_post):
    batch, seq, d = x.shape
    depth = w_in.shape[0]
    t = batch * seq
    off_f = 3 * D_FOX
    off_hg = off_f + HEADS

    qscale = jnp.concatenate([jnp.full((1, D_FOX), LOG2E * HEAD_DIM ** -0.5, F32),
                              jnp.ones((1, 2 * D_FOX), F32)], axis=1)
    ones_hg = jnp.ones((1, 4 * D_HGRN), F32)
    w_in_t = jnp.swapaxes(w_in, 1, 2)
    w_f = jnp.pad(w_in[:, :, off_f:off_hg], ((0, 0), (0, 0), (0, LANES - HEADS))).astype(BF16)
    b_f = jnp.pad(b_fox_f, ((0, 0), (0, LANES - HEADS)))
    conv_b3 = conv_b.reshape(depth, 1, -1)
    w_out16, w_down16 = w_out.astype(BF16), w_down.astype(BF16)
    w_gate16, w_pproj16 = w_ple_gate.astype(BF16), w_ple_proj.astype(BF16)

    h = x.reshape(t, d)
    hn = _prenorm(h, g_mix_pre[0])
    for i in range(depth):
        qkv = _proj(hn, w_in_t, i, 0, 3 * D_FOX, qscale, BF16, name="inproj_qkv")
        zh = _proj(hn, w_in_t, i, off_hg, 4 * D_HGRN, ones_hg, F32, name="inproj_hgrn")
        c3 = _fgate(hn, w_f[i], b_f[i:i + 1], batch, seq)
        y_fox = _attention(qkv.reshape(batch, seq, 3 * D_FOX), c3, batch, seq)
        y_hg = _hgrn(zh, w_hgrn_lb, g_hgrn_out[i], i, batch, seq)

        h, hn = _outproj(y_fox, y_hg, w_out16, i, h, g_mix_post[i], g_ffn_pre[i])
        a = _upproj(hn, w_up, conv_w, conv_b3, i, seq)
        h, hn = _downproj(a, w_down16, i, h, g_ffn_post[i], g_ple_in[i])
        g_next = g_mix_pre[i + 1] if i + 1 < depth else None
        h, hn = _ple(p[i].reshape(t, -1), hn, w_pproj16, w_gate16, i, h, g_ple_post[i], g_next)
    return h.reshape(batch, seq, d)
```

```python
import functools
import math

import jax
import jax.numpy as jnp
from jax import lax
from jax.experimental import pallas as pl
from jax.experimental.pallas import tpu as pltpu

F32 = jnp.float32
BF16 = jnp.bfloat16

D_MODEL = 2048
D_FOX = 1024
D_HGRN = 1024
HEADS = 8
HEAD_DIM = 128
D_FF = 5632
CONV_WIDTH = 3
EPS = 1e-6
LOG2E = math.log2(math.e)
LANES = 128
SUBLANES = 8

HGRN_CHUNK = 64
HGRN_CHUNKS_PER_STEP = 4
ATT_BLOCK = 512
ATT_GROUP = 4
ROW_SUBTILES = 2

VMEM_LIMIT = 56 * 1024 * 1024


def _params(*semantics):
    return pltpu.CompilerParams(dimension_semantics=semantics,
                                vmem_limit_bytes=VMEM_LIMIT)


def _rms(x, g):
    ms = jnp.mean(x * x, axis=-1, keepdims=True)
    return (x * lax.rsqrt(ms + EPS)) * g


def _log_sigmoid(x):
    return jnp.minimum(x, 0.0) - jnp.log(1.0 + jnp.exp(-jnp.abs(x)))


def _resident(shape, index_map):
    return pl.BlockSpec(shape, index_map, pipeline_mode=pl.Buffered(1))


def _proj_kernel(x_ref, w_ref, wtail_ref, s_ref, o_ref, wb_ref, *, shift):
    @pl.when(pl.program_id(1) == 0)
    def _():
        if shift:
            w = jnp.concatenate([w_ref[...], wtail_ref[...]], axis=0)
            wb_ref[...] = w[shift:shift + wb_ref.shape[0], :].astype(BF16)
        else:
            wb_ref[...] = w_ref[...].astype(BF16)

    acc = lax.dot_general(x_ref[...], wb_ref[...], (((1,), (1,)), ((), ())),
                          preferred_element_type=F32)
    o_ref[...] = (acc * s_ref[...]).astype(o_ref.dtype)


def _proj(x, w_t, layer, row0, n, colscale, out_dtype, tm=1024, tn=1024, name="proj"):
    t, k = x.shape
    shift = row0 % tn
    base = row0 - shift
    assert shift in (0, SUBLANES) and n % tn == 0 and tn % SUBLANES == 0
    tail = lambda j, i: (layer, (base + (j + 1) * tn) // SUBLANES if shift else 0, 0)
    return pl.pallas_call(
        functools.partial(_proj_kernel, shift=shift),
        grid=(n // tn, t // tm),
        in_specs=[pl.BlockSpec((tm, k), lambda j, i: (i, 0)),
                  pl.BlockSpec((None, tn, k), lambda j, i: (layer, base // tn + j, 0)),
                  pl.BlockSpec((None, SUBLANES, k), tail),
                  pl.BlockSpec((1, tn), lambda j, i: (0, j))],
        out_specs=pl.BlockSpec((tm, tn), lambda j, i: (i, j)),
        out_shape=jax.ShapeDtypeStruct((t, n), out_dtype),
        scratch_shapes=[pltpu.VMEM((tn, k), BF16)],
        compiler_params=_params("arbitrary", "arbitrary"),
        name=name,
    )(x, w_t, w_t, colscale)


def _fgate_kernel(*refs, tm, normalize):
    if normalize:
        x_ref, g_ref, w_ref, b_ref, o_ref, xn_ref, carry_ref = refs
        xn = _rms(x_ref[...], g_ref[...]).astype(BF16)
        xn_ref[...] = xn
    else:
        x_ref, w_ref, b_ref, o_ref, carry_ref = refs
        xn = x_ref[...]

    @pl.when(pl.program_id(1) == 0)
    def _():
        carry_ref[...] = jnp.zeros_like(carry_ref)

    f = jnp.dot(xn, w_ref[...], preferred_element_type=F32) + b_ref[...]
    c = _log_sigmoid(f)
    row = lax.broadcasted_iota(jnp.int32, c.shape, 0)
    d = 1
    while d < tm:
        c = c + jnp.where(row >= d, pltpu.roll(c, d, 0), 0.0)
        d *= 2
    c = c + carry_ref[0:1, :]
    carry_ref[...] = jnp.broadcast_to(c[tm - 1:tm, :], carry_ref.shape)

    x = c * (-LOG2E)
    hi = x.astype(BF16).astype(F32)
    rest = x - hi
    mid = rest.astype(BF16).astype(F32)
    lo = rest - mid
    lane = lax.broadcasted_iota(jnp.int32, c.shape, 1)
    parts = jnp.where(lane < HEADS, hi,
                      jnp.where(lane < 2 * HEADS, pltpu.roll(mid, HEADS, 1),
                                jnp.where(lane < 3 * HEADS, pltpu.roll(lo, 2 * HEADS, 1), 0.0)))
    o_ref[0] = parts.astype(BF16)


def _fgate(x, w_f, b_f, batch, seq, g_pre=None, tm=512):
    t, k = x.shape
    nt = seq // tm
    normalize = g_pre is not None
    rows = pl.BlockSpec((tm, k), lambda b, s: (b * nt + s, 0))
    in_specs = [rows] + ([pl.BlockSpec((1, k), lambda b, s: (0, 0))] if normalize else []) + [
        pl.BlockSpec((k, LANES), lambda b, s: (0, 0)),
        pl.BlockSpec((1, LANES), lambda b, s: (0, 0))]
    out_specs = [pl.BlockSpec((1, tm, LANES), lambda b, s: (b, s, 0))]
    out_shape = [jax.ShapeDtypeStruct((batch, seq, LANES), BF16)]
    if normalize:
        out_specs.append(rows)
        out_shape.append(jax.ShapeDtypeStruct((t, k), BF16))
    args = (x, g_pre.reshape(1, k), w_f, b_f) if normalize else (x, w_f, b_f)
    outs = pl.pallas_call(
        functools.partial(_fgate_kernel, tm=tm, normalize=normalize),
        grid=(batch, nt),
        in_specs=in_specs,
        out_specs=out_specs,
        out_shape=out_shape,
        scratch_shapes=[pltpu.VMEM((8, LANES), F32)],
        compiler_params=_params("arbitrary", "arbitrary"),
        name="fgate",
    )(*args)
    return tuple(outs) if normalize else outs[0]


def _attn_kernel(qa_ref, qb_ref, k_ref, v_ref, c_ref, o_ref, vt_ref, q_scr, m_ref, acc_ref,
                 *, blk, nb, group):
    hg, ip = pl.program_id(1), pl.program_id(2)
    head_cols = [slice(hh * HEAD_DIM, (hh + 1) * HEAD_DIM) for hh in range(group)]

    @pl.when(ip == 0)
    def _():
        for hh in range(group):
            for r in range(nb):
                vblk = v_ref[0, r * blk:(r + 1) * blk, head_cols[hh]]
                vt_ref[hh * nb + r, 0:HEAD_DIM, :] = vblk.astype(F32).T.astype(BF16)
                vt_ref[hh * nb + r, HEAD_DIM:HEAD_DIM + 8, :] = jnp.ones((8, blk), BF16)

    m_ref[...] = jnp.full(m_ref.shape, -jnp.inf, F32)
    acc_ref[...] = jnp.zeros(acc_ref.shape, F32)

    lane = lax.broadcasted_iota(jnp.int32, (blk, HEAD_DIM), 1)
    for hh in range(group):
        head = hg * group + hh
        gate_lanes = (lane == head) | (lane == head + HEADS) | (lane == head + 2 * HEADS)
        ones = jnp.where(gate_lanes, 1.0, 0.0).astype(BF16)
        for tile, q_ref in enumerate((qa_ref, qb_ref)):
            q_scr[group * tile + hh] = jnp.concatenate([q_ref[0, :, head_cols[hh]], ones], axis=1)

    def item(t):
        if t == nb:
            return 1, nb - 1 - ip, True
        if t >= nb // 2:
            return 1, t - ip - 1, False
        in_b = t > ip
        return in_b.astype(jnp.int32), jnp.where(in_b, t - ip - 1, t), t == ip

    def scores(t):
        tile, kb, diag = item(t)
        ks = pl.multiple_of(kb * blk, blk)
        gate = c_ref[0, pl.ds(ks, blk), :]
        out = []
        for hh in range(group):
            k_aug = jnp.concatenate([k_ref[0, pl.ds(ks, blk), head_cols[hh]], gate], axis=1)
            s = lax.dot_general(k_aug, q_scr[group * tile + hh], (((1,), (1,)), ((), ())),
                                preferred_element_type=F32)
            if diag is not False:
                key = lax.broadcasted_iota(jnp.int32, s.shape, 0)
                qry = lax.broadcasted_iota(jnp.int32, s.shape, 1)
                keep = (key <= qry) if diag is True else ((key <= qry) | (t != ip))
                s = jnp.where(keep, s, -jnp.inf)
            out.append((s, jnp.max(s, axis=0, keepdims=True)))
        return out

    def softmax_pv(t, scored):
        tile, kb, _ = item(t)
        for hh in range(group):
            idx = group * tile + hh
            s, bmax = scored[hh]
            m_old = m_ref[idx]
            m_new = jnp.maximum(m_old, bmax)
            alpha = jnp.exp2(m_old - m_new)
            p = jnp.exp2(s - m_new)
            pv = jnp.dot(vt_ref[hh * nb + kb], p.astype(BF16), preferred_element_type=F32)
            acc_ref[idx] = alpha * acc_ref[idx] + pv
            m_ref[idx] = m_new

    scored = scores(0)
    for t in range(nb + 1):
        ahead = scores(t + 1) if t < nb else None
        softmax_pv(t, scored)
        scored = ahead

    for tile in range(2):
        for hh in range(group):
            acc = acc_ref[group * tile + hh]
            o = (acc[0:HEAD_DIM] / acc[HEAD_DIM:HEAD_DIM + 1]).T
            o_ref[0, 0, tile, :, head_cols[hh]] = o.astype(o_ref.dtype)


def _attention(qkv, gate, batch, seq, blk=ATT_BLOCK, group=ATT_GROUP):
    nb = seq // blk
    width = group * HEAD_DIM
    ngroup = HEADS // group
    return pl.pallas_call(
        functools.partial(_attn_kernel, blk=blk, nb=nb, group=group),
        grid=(batch, ngroup, nb // 2),
        in_specs=[
            pl.BlockSpec((1, blk, width), lambda b, h, i: (b, i, h)),
            pl.BlockSpec((1, blk, width), lambda b, h, i: (b, nb - 1 - i, h)),
            pl.BlockSpec((1, seq, width), lambda b, h, i: (b, 0, ngroup + h)),
            pl.BlockSpec((1, seq, width), lambda b, h, i: (b, 0, 2 * ngroup + h)),
            pl.BlockSpec((1, seq, LANES), lambda b, h, i: (b, 0, 0)),
        ],
        out_specs=pl.BlockSpec((1, 1, 2, blk, width), lambda b, h, i: (b, i, 0, 0, h)),
        out_shape=jax.ShapeDtypeStruct((batch, nb // 2, 2, blk, D_FOX), BF16),
        scratch_shapes=[pltpu.VMEM((group * nb, HEAD_DIM + 8, blk), BF16),
                        pltpu.VMEM((2 * group, blk, 2 * HEAD_DIM), BF16),
                        pltpu.VMEM((2 * group, 1, blk), F32),
                        pltpu.VMEM((2 * group, HEAD_DIM + 8, blk), F32)],
        compiler_params=_params("arbitrary", "arbitrary", "arbitrary"),
        name="fox_attention",
    )(qkv, qkv, qkv, qkv, gate)


def _block_middle_rows(cum, cum_ref, chunk):
    row = lax.broadcasted_iota(jnp.int32, cum.shape, 0)
    spread = lambda r: jnp.broadcast_to(cum_ref[r:r + 1, :], (SUBLANES, cum.shape[-1]))
    out = []
    h = chunk // 2
    while h >= 4:
        reps = max(2 * h // SUBLANES, 1)
        r = jnp.concatenate([spread(b + h - 1) for b in range(0, chunk, 2 * h)
                             for _ in range(reps)], axis=0)
        out.append((h, r))
        h //= 2
    low = jnp.concatenate([spread(b + 1) for b in range(0, chunk, SUBLANES)], axis=0)
    high = jnp.concatenate([spread(b + 5) for b in range(0, chunk, SUBLANES)], axis=0)
    out.append((2, jnp.where(row % SUBLANES < 4, low, high)))
    tiles = cum.reshape(chunk // SUBLANES, SUBLANES, cum.shape[-1])
    above = pltpu.roll(tiles, 1, 1).reshape(cum.shape)
    out.append((1, jnp.where(row % 2 == 1, above, cum)))
    return out


def _hgrn_kernel(q_ref, f_ref, i_ref, g_ref, wlb_ref, go_ref, o_ref, state_ref, cum_ref,
                 *, chunk, layer):
    @pl.when(pl.program_id(1) == 0)
    def _():
        state_ref[...] = jnp.zeros_like(state_ref)

    if layer > 0:
        w = wlb_ref[...]
        ew = jnp.exp(w - jnp.max(w, axis=0, keepdims=True))
        sm = ew / jnp.sum(ew, axis=0, keepdims=True)
        lb_all = sm[1:2]
        for dpt in range(2, layer + 1):
            lb_all = lb_all + sm[dpt:dpt + 1]
        log_lb_all = jnp.log(lb_all)
        log_1mlb_all = jnp.log1p(-lb_all)

    row = lax.broadcasted_iota(jnp.int32, (chunk, HEAD_DIM), 0)
    tile_row = lax.broadcasted_iota(jnp.int32, (chunk // SUBLANES, SUBLANES, HEAD_DIM), 1)
    rr = lax.broadcasted_iota(jnp.int32, (chunk, chunk), 0)
    cc = lax.broadcasted_iota(jnp.int32, (chunk, chunk), 1)
    pair_mask = {}
    side = {}
    h = chunk // 2
    while h >= 1:
        pair_mask[h] = (((rr // (2 * h)) == (cc // (2 * h)))
                        & (rr % (2 * h) >= h) & (cc % (2 * h) < h))
        side[h] = jnp.where(row % (2 * h) >= h, 1.0, -1.0)
        h //= 2

    def scores(rows):
        stage = []
        for hd in range(HEADS):
            sl = slice(hd * HEAD_DIM, (hd + 1) * HEAD_DIM)
            q = q_ref[rows, sl]
            f = f_ref[rows, sl]
            v = i_ref[rows, sl]
            if layer == 0:
                log_f = _log_sigmoid(f)
                k = 1.0 / (1.0 + jnp.exp(f))
            else:
                a = log_lb_all[:, sl]
                b = log_1mlb_all[:, sl] + _log_sigmoid(f)
                log_f = jnp.maximum(a, b) + jnp.log(1.0 + jnp.exp(-jnp.abs(a - b)))
                k = (1.0 - lb_all[:, sl]) * (1.0 / (1.0 + jnp.exp(f)))

            cum = log_f * LOG2E
            tiles = cum.reshape(chunk // SUBLANES, SUBLANES, HEAD_DIM)
            d = 1
            while d < SUBLANES:
                tiles = tiles + jnp.where(tile_row >= d, pltpu.roll(tiles, d, 1), 0.0)
                d *= 2
            offset, offsets = None, [jnp.zeros((1, 1, HEAD_DIM), F32)]
            for tile in range(chunk // SUBLANES - 1):
                total = tiles[tile:tile + 1, SUBLANES - 1:SUBLANES, :]
                offset = total if offset is None else offset + total
                offsets.append(offset)
            cum = (tiles + jnp.concatenate(offsets, axis=0)).reshape(chunk, HEAD_DIM)
            last = cum[chunk - 1:chunk, :]

            sc = jnp.zeros((chunk, chunk), F32)
            q16, k16 = q.astype(BF16), k.astype(BF16)
            cum_ref[hd] = cum
            for h, r in _block_middle_rows(cum, cum_ref.at[hd], chunk):
                decay = jnp.exp2((cum - r) * side[h]).astype(BF16)
                s_l = lax.dot_general(q16 * decay, k16 * decay,
                                      (((1,), (1,)), ((), ())), preferred_element_type=F32)
                sc = jnp.where(pair_mask[h], s_l, sc)
            stage.append((sc.astype(BF16), v.astype(BF16),
                          jnp.sum(q * k, axis=-1, keepdims=True) * v,
                          (q * jnp.exp2(cum)).astype(BF16),
                          (k * jnp.exp2(last - cum)).astype(BF16), jnp.exp2(last)))
        return stage

    def recur(rows, stage):
        for hd in range(HEADS):
            sl = slice(hd * HEAD_DIM, (hd + 1) * HEAD_DIM)
            sc16, v16, o_diag, qd, kd, decay_last = stage[hd]
            st = state_ref[hd]
            o = o_diag + jnp.dot(sc16, v16, preferred_element_type=F32)
            o = o + lax.dot_general(qd, st.astype(BF16), (((1,), (1,)), ((), ())),
                                    preferred_element_type=F32)
            upd = lax.dot_general(v16, kd, (((0,), (0,)), ((), ())),
                                  preferred_element_type=F32)
            state_ref[hd] = st * decay_last + upd

            g = g_ref[rows, sl]
            y = _rms(o, go_ref[...]) * (g * (1.0 / (1.0 + jnp.exp(-g))))
            o_ref[rows, sl] = y.astype(o_ref.dtype)

    n_chunks = q_ref.shape[0] // chunk
    rows = [slice(c * chunk, (c + 1) * chunk) for c in range(n_chunks)]
    stage = scores(rows[0])
    for c in range(n_chunks):
        nxt = scores(rows[c + 1]) if c + 1 < n_chunks else None
        recur(rows[c], stage)
        stage = nxt


def _hgrn(zh, w_lb, g_o, layer, batch, seq, chunk=HGRN_CHUNK,
          chunks_per_step=HGRN_CHUNKS_PER_STEP):
    t = zh.shape[0]
    rows = chunk * chunks_per_step
    ns = seq // rows
    spec = lambda grp: pl.BlockSpec((rows, D_HGRN), lambda b, c: (b * ns + c, grp))
    return pl.pallas_call(
        functools.partial(_hgrn_kernel, chunk=chunk, layer=layer),
        grid=(batch, ns),
        in_specs=[spec(0), spec(1), spec(2), spec(3),
                  pl.BlockSpec(w_lb.shape, lambda b, c: (0, 0)),
                  pl.BlockSpec((1, HEAD_DIM), lambda b, c: (0, 0))],
        out_specs=pl.BlockSpec((rows, D_HGRN), lambda b, c: (b * ns + c, 0)),
        out_shape=jax.ShapeDtypeStruct((t, D_HGRN), BF16),
        scratch_shapes=[pltpu.VMEM((HEADS, HEAD_DIM, HEAD_DIM), F32),
                        pltpu.VMEM((HEADS, chunk, HEAD_DIM), F32)],
        compiler_params=_params("arbitrary", "arbitrary"),
        name="hgrn2",
    )(zh, zh, zh, zh, w_lb, g_o.reshape(1, HEAD_DIM))


def _residual_norm_store(branch, rows, h_ref, gpost_ref, gnext_ref, h_out_ref, hn_out_ref):
    h = h_ref[rows, :] + _rms(branch, gpost_ref[...])
    h_out_ref[rows, :] = h
    if hn_out_ref is not None:
        hn_out_ref[rows, :] = _rms(h, gnext_ref[...]).astype(hn_out_ref.dtype)


def _subtile_rows(tm):
    sub = tm // ROW_SUBTILES
    return [slice(s * sub, (s + 1) * sub) for s in range(ROW_SUBTILES)]


def _outproj_kernel(ya_ref, yb_ref, wa_ref, wb_ref, h_ref, gpost_ref, gnext_ref,
                    h_out_ref, hn_out_ref):
    mixes = []
    for rows in _subtile_rows(h_ref.shape[0]):
        mix = jnp.dot(ya_ref[rows, :], wa_ref[...], preferred_element_type=F32)
        mixes.append(mix + jnp.dot(yb_ref[rows, :], wb_ref[...], preferred_element_type=F32))
    for rows, mix in zip(_subtile_rows(h_ref.shape[0]), mixes):
        _residual_norm_store(mix, rows, h_ref, gpost_ref, gnext_ref, h_out_ref, hn_out_ref)


def _outproj(ya, yb, w_out, layer, h, g_post, g_next):
    t, d = h.shape
    _, half_nb, _, tm, ka = ya.shape
    nb = 2 * half_nb
    kb = yb.shape[1]
    row = lambda width: pl.BlockSpec((tm, width), lambda i: (i, 0))
    gain = _resident((1, d), lambda i: (0, 0))

    def ya_index(i):
        b, tile = i // nb, i % nb
        second = tile >= half_nb
        return (b, jnp.where(second, nb - 1 - tile, tile), second.astype(jnp.int32), 0, 0)

    return pl.pallas_call(
        _outproj_kernel,
        grid=(t // tm,),
        in_specs=[pl.BlockSpec((None, None, None, tm, ka), ya_index), row(kb),
                  _resident((None, ka, d), lambda i: (layer, 0, 0)),
                  _resident((None, kb, d), lambda i: (layer, ka // kb, 0)),
                  row(d), gain, gain],
        out_specs=[row(d), row(d)],
        out_shape=[jax.ShapeDtypeStruct((t, d), F32), jax.ShapeDtypeStruct((t, d), BF16)],
        compiler_params=_params("arbitrary"),
        name="outproj",
    )(ya, yb, w_out, w_out, h, g_post.reshape(1, d), g_next.reshape(1, d))


def _upproj_kernel(x_ref, wg_ref, wu_ref, cwg_ref, cwu_ref, cbg_ref, cbu_ref, o_ref,
                   wgb_ref, wub_ref, ug_ref, uu_ref, *, tm, tiles_per_seq):
    i = pl.program_id(1)

    @pl.when(i == 0)
    def _():
        wgb_ref[...] = wg_ref[...].astype(BF16)
        wub_ref[...] = wu_ref[...].astype(BF16)

    @pl.when(i % tiles_per_seq == 0)
    def _():
        ug_ref[0:8, :] = jnp.zeros((8, ug_ref.shape[1]), F32)
        uu_ref[0:8, :] = jnp.zeros((8, uu_ref.shape[1]), F32)

    x = x_ref[...]
    ug_ref[8:tm + 8, :] = jnp.dot(x, wgb_ref[...], preferred_element_type=F32)
    uu_ref[8:tm + 8, :] = jnp.dot(x, wub_ref[...], preferred_element_type=F32)

    def conv(u_ref, cw_ref, cb_ref):
        acc = cb_ref[...] + cw_ref[2:3, :] * u_ref[8:tm + 8, :]
        acc = acc + cw_ref[1:2, :] * u_ref[7:tm + 7, :]
        return acc + cw_ref[0:1, :] * u_ref[6:tm + 6, :]

    gate = conv(ug_ref, cwg_ref, cbg_ref)
    up = conv(uu_ref, cwu_ref, cbu_ref)
    o_ref[...] = (gate * (1.0 / (1.0 + jnp.exp(-gate))) * up).astype(o_ref.dtype)

    ug_ref[0:8, :] = ug_ref[tm:tm + 8, :]
    uu_ref[0:8, :] = uu_ref[tm:tm + 8, :]


def _upproj(xn, w_up, conv_w, conv_b, layer, seq, tm=1024, tn=512):
    t, k = xn.shape
    nj = D_FF // tn
    lay = lambda r, off: pl.BlockSpec((None, r, tn), lambda j, i: (layer, 0, j + off))
    return pl.pallas_call(
        functools.partial(_upproj_kernel, tm=tm, tiles_per_seq=seq // tm),
        grid=(nj, t // tm),
        in_specs=[pl.BlockSpec((tm, k), lambda j, i: (i, 0)),
                  lay(k, 0), lay(k, nj),
                  lay(CONV_WIDTH, 0), lay(CONV_WIDTH, nj),
                  lay(1, 0), lay(1, nj)],
        out_specs=pl.BlockSpec((tm, tn), lambda j, i: (i, j)),
        out_shape=jax.ShapeDtypeStruct((t, D_FF), BF16),
        scratch_shapes=[pltpu.VMEM((k, tn), BF16), pltpu.VMEM((k, tn), BF16),
                        pltpu.VMEM((tm + 8, tn), F32), pltpu.VMEM((tm + 8, tn), F32)],
        compiler_params=_params("arbitrary", "arbitrary"),
        name="upproj_conv",
    )(xn, w_up, w_up, conv_w, conv_w, conv_b, conv_b)


def _downproj_kernel(a_ref, w_ref, h_ref, gpost_ref, gnext_ref, h_out_ref, hn_out_ref):
    ff = jnp.dot(a_ref[...], w_ref[...], preferred_element_type=F32)
    _residual_norm_store(ff, slice(None), h_ref, gpost_ref, gnext_ref, h_out_ref, hn_out_ref)


def _downproj(a, w_down, layer, h, g_post, g_next, tm=256):
    t, d = h.shape
    k = a.shape[1]
    row = lambda width: pl.BlockSpec((tm, width), lambda i: (i, 0))
    gain = _resident((1, d), lambda i: (0, 0))
    return pl.pallas_call(
        _downproj_kernel,
        grid=(t // tm,),
        in_specs=[row(k), _resident((None, k, d), lambda i: (layer, 0, 0)), row(d), gain, gain],
        out_specs=[row(d), row(d)],
        out_shape=[jax.ShapeDtypeStruct((t, d), F32), jax.ShapeDtypeStruct((t, d), BF16)],
        compiler_params=_params("arbitrary"),
        name="downproj",
    )(a, w_down, h, g_post.reshape(1, d), g_next.reshape(1, d))


def _ple_kernel(p_ref, hn_ref, wp_ref, wg_ref, h_ref, gpost_ref, gnext_ref,
                h_out_ref, hn_out_ref=None):
    pairs = []
    for rows in _subtile_rows(h_ref.shape[0]):
        e = jnp.dot(p_ref[rows, :].astype(BF16), wp_ref[...], preferred_element_type=F32)
        z = jnp.dot(hn_ref[rows, :], wg_ref[...], preferred_element_type=F32)
        pairs.append((e, z))
    for rows, (e, z) in zip(_subtile_rows(h_ref.shape[0]), pairs):
        gated = e * (1.0 / (1.0 + jnp.exp(-z)))
        _residual_norm_store(gated, rows, h_ref, gpost_ref, gnext_ref, h_out_ref, hn_out_ref)


def _ple(p, hn, w_proj, w_gate, layer, h, g_post, g_next, tm=512):
    t, d = h.shape
    kp = p.shape[1]
    row = lambda width: pl.BlockSpec((tm, width), lambda i: (i, 0))
    gain = _resident((1, d), lambda i: (0, 0))
    last = g_next is None
    out_shape = [jax.ShapeDtypeStruct((t, d), F32)]
    if not last:
        out_shape.append(jax.ShapeDtypeStruct((t, d), BF16))
    outs = pl.pallas_call(
        _ple_kernel,
        grid=(t // tm,),
        in_specs=[row(kp), row(d),
                  _resident((None, kp, d), lambda i: (layer, 0, 0)),
                  _resident((None, d, d), lambda i: (layer, 0, 0)),
                  row(d), gain, gain],
        out_specs=[row(d)] * len(out_shape),
        out_shape=out_shape,
        compiler_params=_params("arbitrary"),
        name="ple",
    )(p, hn, w_proj, w_gate, h, g_post.reshape(1, d),
      (g_post if last else g_next).reshape(1, d))
    return (outs[0], None) if last else tuple(outs)


def kernel(x, p, g_mix_pre, w_in, b_fox_f, w_hgrn_lb, g_hgrn_out, w_out, g_mix_post,
           g_ffn_pre, w_up, conv_w, conv_b, w_down, g_ffn_post,
           g_ple_in, w_ple_gate, w_ple_proj, g_ple_post):
    batch, seq, d = x.shape
    depth = w_in.shape[0]
    t = batch * seq
    off_f = 3 * D_FOX
    off_hg = off_f + HEADS

    qscale = jnp.concatenate([jnp.full((1, D_FOX), LOG2E * HEAD_DIM ** -0.5, F32),
                              jnp.ones((1, 2 * D_FOX), F32)], axis=1)
    ones_hg = jnp.ones((1, 4 * D_HGRN), F32)
    w_in_t = jnp.swapaxes(w_in, 1, 2)
    w_f = jnp.pad(w_in[:, :, off_f:off_hg], ((0, 0), (0, 0), (0, LANES - HEADS))).astype(BF16)
    b_f = jnp.pad(b_fox_f, ((0, 0), (0, LANES - HEADS)))
    conv_b3 = conv_b.reshape(depth, 1, -1)
    w_out16, w_down16 = w_out.astype(BF16), w_down.astype(BF16)
    w_gate16, w_pproj16 = w_ple_gate.astype(BF16), w_ple_proj.astype(BF16)

    h = x.reshape(t, d)
    hn = None
    for i in range(depth):
        if hn is None:
            c3, hn = _fgate(h, w_f[i], b_f[i:i + 1], batch, seq, g_pre=g_mix_pre[i])
        else:
            c3 = _fgate(hn, w_f[i], b_f[i:i + 1], batch, seq)
        qkv = _proj(hn, w_in_t, i, 0, 3 * D_FOX, qscale, BF16, name="inproj_qkv")
        zh = _proj(hn, w_in_t, i, off_hg, 4 * D_HGRN, ones_hg, F32, name="inproj_hgrn")
        y_fox = _attention(qkv.reshape(batch, seq, 3 * D_FOX), c3, batch, seq)
        y_hg = _hgrn(zh, w_hgrn_lb, g_hgrn_out[i], i, batch, seq)

        h, hn = _outproj(y_fox, y_hg, w_out16, i, h, g_mix_post[i], g_ffn_pre[i])
        a = _upproj(hn, w_up, conv_w, conv_b3, i, seq)
        h, hn = _downproj(a, w_down16, i, h, g_ffn_post[i], g_ple_in[i])
        g_next = g_mix_pre[i + 1] if i + 1 < depth else None
        h, hn = _ple(p[i].reshape(t, -1), hn, w_pproj16, w_gate16, i, h, g_ple_post[i], g_next)
    return h.reshape(batch, seq, d)
```

```python
import functools
import math

import jax
import jax.numpy as jnp
from jax import lax
from jax.experimental import pallas as pl
from jax.experimental.pallas import tpu as pltpu

F32 = jnp.float32
BF16 = jnp.bfloat16

D_MODEL = 2048
D_FOX = 1024
D_HGRN = 1024
HEADS = 8
HEAD_DIM = 128
D_FF = 5632
CONV_WIDTH = 3
EPS = 1e-6
LOG2E = math.log2(math.e)
LANES = 128
SUBLANES = 8
HALO = SUBLANES
DEN_ROWS = SUBLANES

HGRN_CHUNK = 64
HGRN_CHUNKS_PER_STEP = 8
ATT_BLOCK = 512
ATT_GROUP = 4
ROW_SUBTILES = 4

V7X_VMEM_BYTES = 64 * 1024 * 1024
VMEM_LIMIT = V7X_VMEM_BYTES - 8 * 1024 * 1024


def _params(*semantics):
    return pltpu.CompilerParams(dimension_semantics=semantics,
                                vmem_limit_bytes=VMEM_LIMIT)


def _rms(x, g):
    ms = jnp.mean(x * x, axis=-1, keepdims=True)
    return (x * lax.rsqrt(ms + EPS)) * g


def _log_sigmoid(x):
    return jnp.minimum(x, 0.0) - jnp.log(1.0 + jnp.exp(-jnp.abs(x)))


def _resident(shape, index_map):
    return pl.BlockSpec(shape, index_map, pipeline_mode=pl.Buffered(1))


def _proj_kernel(x_ref, w_ref, wtail_ref, s_ref, o_ref, wb_ref, *, shift):
    @pl.when(pl.program_id(1) == 0)
    def _():
        if shift:
            w = jnp.concatenate([w_ref[...], wtail_ref[...]], axis=0)
            wb_ref[...] = w[shift:shift + wb_ref.shape[0], :].astype(BF16)
        else:
            wb_ref[...] = w_ref[...].astype(BF16)

    acc = lax.dot_general(x_ref[...], wb_ref[...], (((1,), (1,)), ((), ())),
                          preferred_element_type=F32)
    o_ref[...] = (acc * s_ref[...]).astype(o_ref.dtype)


def _proj(x, w_t, layer, row0, n, colscale, out_dtype, tm=1024, tn=1024, name="proj"):
    t, k = x.shape
    shift = row0 % tn
    base = row0 - shift
    assert shift in (0, SUBLANES) and n % tn == 0 and tn % SUBLANES == 0
    tail = lambda j, i: (layer, (base + (j + 1) * tn) // SUBLANES if shift else 0, 0)
    return pl.pallas_call(
        functools.partial(_proj_kernel, shift=shift),
        grid=(n // tn, t // tm),
        in_specs=[pl.BlockSpec((tm, k), lambda j, i: (i, 0)),
                  pl.BlockSpec((None, tn, k), lambda j, i: (layer, base // tn + j, 0)),
                  pl.BlockSpec((None, SUBLANES, k), tail),
                  pl.BlockSpec((1, tn), lambda j, i: (0, j))],
        out_specs=pl.BlockSpec((tm, tn), lambda j, i: (i, j)),
        out_shape=jax.ShapeDtypeStruct((t, n), out_dtype),
        scratch_shapes=[pltpu.VMEM((tn, k), BF16)],
        compiler_params=_params("arbitrary", "arbitrary"),
        name=name,
    )(x, w_t, w_t, colscale)


def _fgate_kernel(*refs, tm, normalize):
    if normalize:
        x_ref, g_ref, w_ref, b_ref, o_ref, xn_ref, carry_ref = refs
        xn = _rms(x_ref[...], g_ref[...]).astype(BF16)
        xn_ref[...] = xn
    else:
        x_ref, w_ref, b_ref, o_ref, carry_ref = refs
        xn = x_ref[...]

    @pl.when(pl.program_id(1) == 0)
    def _():
        carry_ref[...] = jnp.zeros_like(carry_ref)

    f = jnp.dot(xn, w_ref[...], preferred_element_type=F32) + b_ref[...]
    c = _log_sigmoid(f)
    row = lax.broadcasted_iota(jnp.int32, c.shape, 0)
    d = 1
    while d < tm:
        c = c + jnp.where(row >= d, pltpu.roll(c, d, 0), 0.0)
        d *= 2
    c = c + carry_ref[0:1, :]
    carry_ref[...] = jnp.broadcast_to(c[tm - 1:tm, :], carry_ref.shape)

    x = c * (-LOG2E)
    hi = x.astype(BF16).astype(F32)
    rest = x - hi
    mid = rest.astype(BF16).astype(F32)
    lo = rest - mid
    lane = lax.broadcasted_iota(jnp.int32, c.shape, 1)
    parts = jnp.where(lane < HEADS, hi,
                      jnp.where(lane < 2 * HEADS, pltpu.roll(mid, HEADS, 1),
                                jnp.where(lane < 3 * HEADS, pltpu.roll(lo, 2 * HEADS, 1), 0.0)))
    o_ref[0] = parts.astype(BF16)


def _fgate(x, w_f, b_f, batch, seq, g_pre=None, tm=512):
    t, k = x.shape
    nt = seq // tm
    normalize = g_pre is not None
    rows = pl.BlockSpec((tm, k), lambda b, s: (b * nt + s, 0))
    in_specs = [rows] + ([pl.BlockSpec((1, k), lambda b, s: (0, 0))] if normalize else []) + [
        pl.BlockSpec((k, LANES), lambda b, s: (0, 0)),
        pl.BlockSpec((1, LANES), lambda b, s: (0, 0))]
    out_specs = [pl.BlockSpec((1, tm, LANES), lambda b, s: (b, s, 0))]
    out_shape = [jax.ShapeDtypeStruct((batch, seq, LANES), BF16)]
    if normalize:
        out_specs.append(rows)
        out_shape.append(jax.ShapeDtypeStruct((t, k), BF16))
    args = (x, g_pre.reshape(1, k), w_f, b_f) if normalize else (x, w_f, b_f)
    outs = pl.pallas_call(
        functools.partial(_fgate_kernel, tm=tm, normalize=normalize),
        grid=(batch, nt),
        in_specs=in_specs,
        out_specs=out_specs,
        out_shape=out_shape,
        scratch_shapes=[pltpu.VMEM((SUBLANES, LANES), F32)],
        compiler_params=_params("arbitrary", "arbitrary"),
        name="fgate",
    )(*args)
    return tuple(outs) if normalize else outs[0]


def _attn_kernel(qa_ref, qb_ref, k_ref, v_ref, c_ref, o_ref, vt_ref, q_scr, m_ref, acc_ref,
                 *, blk, nb, group):
    hg, ip = pl.program_id(1), pl.program_id(2)
    head_cols = [slice(hh * HEAD_DIM, (hh + 1) * HEAD_DIM) for hh in range(group)]

    @pl.when(ip == 0)
    def _():
        for hh in range(group):
            for r in range(nb):
                vblk = v_ref[0, r * blk:(r + 1) * blk, head_cols[hh]]
                vt_ref[hh * nb + r, 0:HEAD_DIM, :] = vblk.astype(F32).T.astype(BF16)
                vt_ref[hh * nb + r, HEAD_DIM:HEAD_DIM + DEN_ROWS, :] = jnp.ones((DEN_ROWS, blk),
                                                                                BF16)

    m_ref[...] = jnp.full(m_ref.shape, -jnp.inf, F32)
    acc_ref[...] = jnp.zeros(acc_ref.shape, F32)

    lane = lax.broadcasted_iota(jnp.int32, (blk, HEAD_DIM), 1)
    for hh in range(group):
        head = hg * group + hh
        gate_lanes = (lane == head) | (lane == head + HEADS) | (lane == head + 2 * HEADS)
        ones = jnp.where(gate_lanes, 1.0, 0.0).astype(BF16)
        for tile, q_ref in enumerate((qa_ref, qb_ref)):
            q_scr[group * tile + hh] = jnp.concatenate([q_ref[0, :, head_cols[hh]], ones], axis=1)

    def item(t):
        if t == nb:
            return 1, nb - 1 - ip, True
        if t >= nb // 2:
            return 1, t - ip - 1, False
        in_b = t > ip
        return in_b.astype(jnp.int32), jnp.where(in_b, t - ip - 1, t), t == ip

    def scores(t):
        tile, kb, diag = item(t)
        ks = pl.multiple_of(kb * blk, blk)
        gate = c_ref[0, pl.ds(ks, blk), :]
        out = []
        for hh in range(group):
            k_aug = jnp.concatenate([k_ref[0, pl.ds(ks, blk), head_cols[hh]], gate], axis=1)
            s = lax.dot_general(k_aug, q_scr[group * tile + hh], (((1,), (1,)), ((), ())),
                                preferred_element_type=F32)
            if diag is not False:
                key = lax.broadcasted_iota(jnp.int32, s.shape, 0)
                qry = lax.broadcasted_iota(jnp.int32, s.shape, 1)
                keep = (key <= qry) if diag is True else ((key <= qry) | (t != ip))
                s = jnp.where(keep, s, -jnp.inf)
            out.append((s, jnp.max(s, axis=0, keepdims=True)))
        return out

    def softmax_pv(t, scored):
        tile, kb, _ = item(t)
        for hh in range(group):
            idx = group * tile + hh
            s, bmax = scored[hh]
            m_old = m_ref[idx]
            m_new = jnp.maximum(m_old, bmax)
            alpha = jnp.exp2(m_old - m_new)
            p = jnp.exp2(s - m_new)
            pv = jnp.dot(vt_ref[hh * nb + kb], p.astype(BF16), preferred_element_type=F32)
            acc_ref[idx] = alpha * acc_ref[idx] + pv
            m_ref[idx] = m_new

    scored = scores(0)
    for t in range(nb + 1):
        ahead = scores(t + 1) if t < nb else None
        softmax_pv(t, scored)
        scored = ahead

    for tile in range(2):
        for hh in range(group):
            acc = acc_ref[group * tile + hh]
            o = (acc[0:HEAD_DIM] / acc[HEAD_DIM:HEAD_DIM + 1]).T
            o_ref[0, 0, tile, :, head_cols[hh]] = o.astype(o_ref.dtype)


def _attention(qkv, gate, batch, seq, blk=ATT_BLOCK, group=ATT_GROUP):
    nb = seq // blk
    width = group * HEAD_DIM
    ngroup = HEADS // group
    return pl.pallas_call(
        functools.partial(_attn_kernel, blk=blk, nb=nb, group=group),
        grid=(batch, ngroup, nb // 2),
        in_specs=[
            pl.BlockSpec((1, blk, width), lambda b, h, i: (b, i, h)),
            pl.BlockSpec((1, blk, width), lambda b, h, i: (b, nb - 1 - i, h)),
            pl.BlockSpec((1, seq, width), lambda b, h, i: (b, 0, ngroup + h)),
            pl.BlockSpec((1, seq, width), lambda b, h, i: (b, 0, 2 * ngroup + h)),
            pl.BlockSpec((1, seq, LANES), lambda b, h, i: (b, 0, 0)),
        ],
        out_specs=pl.BlockSpec((1, 1, 2, blk, width), lambda b, h, i: (b, i, 0, 0, h)),
        out_shape=jax.ShapeDtypeStruct((batch, nb // 2, 2, blk, D_FOX), BF16),
        scratch_shapes=[pltpu.VMEM((group * nb, HEAD_DIM + DEN_ROWS, blk), BF16),
                        pltpu.VMEM((2 * group, blk, 2 * HEAD_DIM), BF16),
                        pltpu.VMEM((2 * group, 1, blk), F32),
                        pltpu.VMEM((2 * group, HEAD_DIM + DEN_ROWS, blk), F32)],
        compiler_params=_params("arbitrary", "arbitrary", "arbitrary"),
        name="fox_attention",
    )(qkv, qkv, qkv, qkv, gate)


def _block_middle_rows(cum, cum_ref, chunk):
    row = lax.broadcasted_iota(jnp.int32, cum.shape, 0)
    spread = lambda r: jnp.broadcast_to(cum_ref[r:r + 1, :], (SUBLANES, cum.shape[-1]))
    out = []
    h = chunk // 2
    while h >= 4:
        reps = max(2 * h // SUBLANES, 1)
        r = jnp.concatenate([spread(b + h - 1) for b in range(0, chunk, 2 * h)
                             for _ in range(reps)], axis=0)
        out.append((h, r))
        h //= 2
    low = jnp.concatenate([spread(b + 1) for b in range(0, chunk, SUBLANES)], axis=0)
    high = jnp.concatenate([spread(b + 5) for b in range(0, chunk, SUBLANES)], axis=0)
    out.append((2, jnp.where(row % SUBLANES < 4, low, high)))
    tiles = cum.reshape(chunk // SUBLANES, SUBLANES, cum.shape[-1])
    above = pltpu.roll(tiles, 1, 1).reshape(cum.shape)
    out.append((1, jnp.where(row % 2 == 1, above, cum)))
    return out


def _hgrn_kernel(q_ref, f_ref, i_ref, g_ref, wlb_ref, go_ref, o_ref, state_ref, cum_ref,
                 *, chunk, layer):
    @pl.when(pl.program_id(1) == 0)
    def _():
        state_ref[...] = jnp.zeros_like(state_ref)

    if layer > 0:
        w = wlb_ref[...]
        ew = jnp.exp(w - jnp.max(w, axis=0, keepdims=True))
        sm = ew / jnp.sum(ew, axis=0, keepdims=True)
        lb_all = sm[1:2]
        for dpt in range(2, layer + 1):
            lb_all = lb_all + sm[dpt:dpt + 1]
        log_lb_all = jnp.log(lb_all)
        log_1mlb_all = jnp.log1p(-lb_all)

    row = lax.broadcasted_iota(jnp.int32, (chunk, HEAD_DIM), 0)
    tile_row = lax.broadcasted_iota(jnp.int32, (chunk // SUBLANES, SUBLANES, HEAD_DIM), 1)
    rr = lax.broadcasted_iota(jnp.int32, (chunk, chunk), 0)
    cc = lax.broadcasted_iota(jnp.int32, (chunk, chunk), 1)
    pair_mask = {}
    side = {}
    h = chunk // 2
    while h >= 1:
        pair_mask[h] = (((rr // (2 * h)) == (cc // (2 * h)))
                        & (rr % (2 * h) >= h) & (cc % (2 * h) < h))
        side[h] = jnp.where(row % (2 * h) >= h, 1.0, -1.0)
        h //= 2

    def scores(rows):
        stage = []
        for hd in range(HEADS):
            sl = slice(hd * HEAD_DIM, (hd + 1) * HEAD_DIM)
            q = q_ref[rows, sl]
            f = f_ref[rows, sl]
            v = i_ref[rows, sl]
            if layer == 0:
                log_f = _log_sigmoid(f)
                k = 1.0 / (1.0 + jnp.exp(f))
            else:
                a = log_lb_all[:, sl]
                b = log_1mlb_all[:, sl] + _log_sigmoid(f)
                log_f = jnp.maximum(a, b) + jnp.log(1.0 + jnp.exp(-jnp.abs(a - b)))
                k = (1.0 - lb_all[:, sl]) * (1.0 / (1.0 + jnp.exp(f)))

            cum = log_f * LOG2E
            tiles = cum.reshape(chunk // SUBLANES, SUBLANES, HEAD_DIM)
            d = 1
            while d < SUBLANES:
                tiles = tiles + jnp.where(tile_row >= d, pltpu.roll(tiles, d, 1), 0.0)
                d *= 2
            offset, offsets = None, [jnp.zeros((1, 1, HEAD_DIM), F32)]
            for tile in range(chunk // SUBLANES - 1):
                total = tiles[tile:tile + 1, SUBLANES - 1:SUBLANES, :]
                offset = total if offset is None else offset + total
                offsets.append(offset)
            cum = (tiles + jnp.concatenate(offsets, axis=0)).reshape(chunk, HEAD_DIM)
            last = cum[chunk - 1:chunk, :]

            sc = jnp.zeros((chunk, chunk), F32)
            q16, k16 = q.astype(BF16), k.astype(BF16)
            cum_ref[hd] = cum
            for h, r in _block_middle_rows(cum, cum_ref.at[hd], chunk):
                decay = jnp.exp2((cum - r) * side[h]).astype(BF16)
                s_l = lax.dot_general(q16 * decay, k16 * decay,
                                      (((1,), (1,)), ((), ())), preferred_element_type=F32)
                sc = jnp.where(pair_mask[h], s_l, sc)
            stage.append((sc.astype(BF16), v.astype(BF16),
                          jnp.sum(q * k, axis=-1, keepdims=True) * v,
                          (q * jnp.exp2(cum)).astype(BF16),
                          (k * jnp.exp2(last - cum)).astype(BF16), jnp.exp2(last)))
        return stage

    def recur(rows, stage):
        for hd in range(HEADS):
            sl = slice(hd * HEAD_DIM, (hd + 1) * HEAD_DIM)
            sc16, v16, o_diag, qd, kd, decay_last = stage[hd]
            st = state_ref[hd]
            o = o_diag + jnp.dot(sc16, v16, preferred_element_type=F32)
            o = o + lax.dot_general(qd, st.astype(BF16), (((1,), (1,)), ((), ())),
                                    preferred_element_type=F32)
            upd = lax.dot_general(v16, kd, (((0,), (0,)), ((), ())),
                                  preferred_element_type=F32)
            state_ref[hd] = st * decay_last + upd

            g = g_ref[rows, sl]
            y = _rms(o, go_ref[...]) * (g * (1.0 / (1.0 + jnp.exp(-g))))
            o_ref[rows, sl] = y.astype(o_ref.dtype)

    n_chunks = q_ref.shape[0] // chunk
    rows = [slice(c * chunk, (c + 1) * chunk) for c in range(n_chunks)]
    stage = scores(rows[0])
    for c in range(n_chunks):
        nxt = scores(rows[c + 1]) if c + 1 < n_chunks else None
        recur(rows[c], stage)
        stage = nxt


def _hgrn(zh, w_lb, g_o, layer, batch, seq, chunk=HGRN_CHUNK,
          chunks_per_step=HGRN_CHUNKS_PER_STEP):
    t = zh.shape[0]
    rows = chunk * chunks_per_step
    ns = seq // rows
    spec = lambda grp: pl.BlockSpec((rows, D_HGRN), lambda b, c: (b * ns + c, grp))
    return pl.pallas_call(
        functools.partial(_hgrn_kernel, chunk=chunk, layer=layer),
        grid=(batch, ns),
        in_specs=[spec(0), spec(1), spec(2), spec(3),
                  pl.BlockSpec(w_lb.shape, lambda b, c: (0, 0)),
                  pl.BlockSpec((1, HEAD_DIM), lambda b, c: (0, 0))],
        out_specs=pl.BlockSpec((rows, D_HGRN), lambda b, c: (b * ns + c, 0)),
        out_shape=jax.ShapeDtypeStruct((t, D_HGRN), BF16),
        scratch_shapes=[pltpu.VMEM((HEADS, HEAD_DIM, HEAD_DIM), F32),
                        pltpu.VMEM((HEADS, chunk, HEAD_DIM), F32)],
        compiler_params=_params("arbitrary", "arbitrary"),
        name="hgrn2",
    )(zh, zh, zh, zh, w_lb, g_o.reshape(1, HEAD_DIM))


def _residual_norm_store(branch, rows, h_ref, gpost_ref, gnext_ref, h_out_ref, hn_out_ref):
    h = h_ref[rows, :] + _rms(branch, gpost_ref[...])
    h_out_ref[rows, :] = h
    if hn_out_ref is not None:
        hn_out_ref[rows, :] = _rms(h, gnext_ref[...]).astype(hn_out_ref.dtype)


def _subtile_rows(tm):
    sub = tm // ROW_SUBTILES
    return [slice(s * sub, (s + 1) * sub) for s in range(ROW_SUBTILES)]


def _outproj_kernel(ya_ref, yb_ref, wa_ref, wb_ref, h_ref, gpost_ref, gnext_ref,
                    h_out_ref, hn_out_ref):
    mixes = []
    for rows in _subtile_rows(h_ref.shape[0]):
        mix = jnp.dot(ya_ref[rows, :], wa_ref[...], preferred_element_type=F32)
        mixes.append(mix + jnp.dot(yb_ref[rows, :], wb_ref[...], preferred_element_type=F32))
    for rows, mix in zip(_subtile_rows(h_ref.shape[0]), mixes):
        _residual_norm_store(mix, rows, h_ref, gpost_ref, gnext_ref, h_out_ref, hn_out_ref)


def _outproj(ya, yb, w_out, layer, h, g_post, g_next):
    t, d = h.shape
    _, half_nb, _, tm, ka = ya.shape
    nb = 2 * half_nb
    kb = yb.shape[1]
    row = lambda width: pl.BlockSpec((tm, width), lambda i: (i, 0))
    gain = _resident((1, d), lambda i: (0, 0))

    def ya_index(i):
        b, tile = i // nb, i % nb
        second = tile >= half_nb
        return (b, jnp.where(second, nb - 1 - tile, tile), second.astype(jnp.int32), 0, 0)

    return pl.pallas_call(
        _outproj_kernel,
        grid=(t // tm,),
        in_specs=[pl.BlockSpec((None, None, None, tm, ka), ya_index), row(kb),
                  _resident((None, ka, d), lambda i: (layer, 0, 0)),
                  _resident((None, kb, d), lambda i: (layer, ka // kb, 0)),
                  row(d), gain, gain],
        out_specs=[row(d), row(d)],
        out_shape=[jax.ShapeDtypeStruct((t, d), F32), jax.ShapeDtypeStruct((t, d), BF16)],
        compiler_params=_params("arbitrary"),
        name="outproj",
    )(ya, yb, w_out, w_out, h, g_post.reshape(1, d), g_next.reshape(1, d))


def _upproj_kernel(x_ref, wg_ref, wu_ref, cwg_ref, cwu_ref, cbg_ref, cbu_ref, o_ref,
                   wgb_ref, wub_ref, ug_ref, uu_ref, *, tm, tiles_per_seq):
    i = pl.program_id(1)

    @pl.when(i == 0)
    def _():
        wgb_ref[...] = wg_ref[...].astype(BF16)
        wub_ref[...] = wu_ref[...].astype(BF16)

    @pl.when(i % tiles_per_seq == 0)
    def _():
        ug_ref[0:HALO, :] = jnp.zeros((HALO, ug_ref.shape[1]), F32)
        uu_ref[0:HALO, :] = jnp.zeros((HALO, uu_ref.shape[1]), F32)

    x = x_ref[...]
    ug_ref[HALO:tm + HALO, :] = jnp.dot(x, wgb_ref[...], preferred_element_type=F32)
    uu_ref[HALO:tm + HALO, :] = jnp.dot(x, wub_ref[...], preferred_element_type=F32)

    def conv(u_ref, cw_ref, cb_ref):
        acc = cb_ref[...]
        for j in reversed(range(CONV_WIDTH)):
            back = CONV_WIDTH - 1 - j
            acc = acc + cw_ref[j:j + 1, :] * u_ref[HALO - back:tm + HALO - back, :]
        return acc

    gate = conv(ug_ref, cwg_ref, cbg_ref)
    up = conv(uu_ref, cwu_ref, cbu_ref)
    o_ref[...] = (gate * (1.0 / (1.0 + jnp.exp(-gate))) * up).astype(o_ref.dtype)

    ug_ref[0:HALO, :] = ug_ref[tm:tm + HALO, :]
    uu_ref[0:HALO, :] = uu_ref[tm:tm + HALO, :]


def _upproj(xn, w_up, conv_w, conv_b, layer, seq, tm=1024, tn=512):
    t, k = xn.shape
    nj = D_FF // tn
    lay = lambda r, off: pl.BlockSpec((None, r, tn), lambda j, i: (layer, 0, j + off))
    return pl.pallas_call(
        functools.partial(_upproj_kernel, tm=tm, tiles_per_seq=seq // tm),
        grid=(nj, t // tm),
        in_specs=[pl.BlockSpec((tm, k), lambda j, i: (i, 0)),
                  lay(k, 0), lay(k, nj),
                  lay(CONV_WIDTH, 0), lay(CONV_WIDTH, nj),
                  lay(1, 0), lay(1, nj)],
        out_specs=pl.BlockSpec((tm, tn), lambda j, i: (i, j)),
        out_shape=jax.ShapeDtypeStruct((t, D_FF), BF16),
        scratch_shapes=[pltpu.VMEM((k, tn), BF16), pltpu.VMEM((k, tn), BF16),
                        pltpu.VMEM((tm + HALO, tn), F32), pltpu.VMEM((tm + HALO, tn), F32)],
        compiler_params=_params("arbitrary", "arbitrary"),
        name="upproj_conv",
    )(xn, w_up, w_up, conv_w, conv_w, conv_b, conv_b)


def _downproj_kernel(a_ref, w_ref, h_ref, gpost_ref, gnext_ref, h_out_ref, hn_out_ref):
    ff = jnp.dot(a_ref[...], w_ref[...], preferred_element_type=F32)
    _residual_norm_store(ff, slice(None), h_ref, gpost_ref, gnext_ref, h_out_ref, hn_out_ref)


def _downproj(a, w_down, layer, h, g_post, g_next, tm=256):
    t, d = h.shape
    k = a.shape[1]
    row = lambda width: pl.BlockSpec((tm, width), lambda i: (i, 0))
    gain = _resident((1, d), lambda i: (0, 0))
    return pl.pallas_call(
        _downproj_kernel,
        grid=(t // tm,),
        in_specs=[row(k), _resident((None, k, d), lambda i: (layer, 0, 0)), row(d), gain, gain],
        out_specs=[row(d), row(d)],
        out_shape=[jax.ShapeDtypeStruct((t, d), F32), jax.ShapeDtypeStruct((t, d), BF16)],
        compiler_params=_params("arbitrary"),
        name="downproj",
    )(a, w_down, h, g_post.reshape(1, d), g_next.reshape(1, d))


def _ple_kernel(p_ref, hn_ref, wp_ref, wg_ref, h_ref, gpost_ref, gnext_ref,
                h_out_ref, hn_out_ref=None):
    pairs = []
    for rows in _subtile_rows(h_ref.shape[0]):
        e = jnp.dot(p_ref[rows, :].astype(BF16), wp_ref[...], preferred_element_type=F32)
        z = jnp.dot(hn_ref[rows, :], wg_ref[...], preferred_element_type=F32)
        pairs.append((e, z))
    for rows, (e, z) in zip(_subtile_rows(h_ref.shape[0]), pairs):
        gated = e * (1.0 / (1.0 + jnp.exp(-z)))
        _residual_norm_store(gated, rows, h_ref, gpost_ref, gnext_ref, h_out_ref, hn_out_ref)


def _ple(p, hn, w_proj, w_gate, layer, h, g_post, g_next, tm=512):
    t, d = h.shape
    kp = p.shape[1]
    row = lambda width: pl.BlockSpec((tm, width), lambda i: (i, 0))
    gain = _resident((1, d), lambda i: (0, 0))
    last = g_next is None
    out_shape = [jax.ShapeDtypeStruct((t, d), F32)]
    if not last:
        out_shape.append(jax.ShapeDtypeStruct((t, d), BF16))
    outs = pl.pallas_call(
        _ple_kernel,
        grid=(t // tm,),
        in_specs=[row(kp), row(d),
                  _resident((None, kp, d), lambda i: (layer, 0, 0)),
                  _resident((None, d, d), lambda i: (layer, 0, 0)),
                  row(d), gain, gain],
        out_specs=[row(d)] * len(out_shape),
        out_shape=out_shape,
        compiler_params=_params("arbitrary"),
        name="ple",
    )(p, hn, w_proj, w_gate, h, g_post.reshape(1, d),
      (g_post if last else g_next).reshape(1, d))
    return (outs[0], None) if last else tuple(outs)


def kernel(x, p, g_mix_pre, w_in, b_fox_f, w_hgrn_lb, g_hgrn_out, w_out, g_mix_post,
           g_ffn_pre, w_up, conv_w, conv_b, w_down, g_ffn_post,
           g_ple_in, w_ple_gate, w_ple_proj, g_ple_post):
    batch, seq, d = x.shape
    depth = w_in.shape[0]
    t = batch * seq
    off_f = 3 * D_FOX
    off_hg = off_f + HEADS

    qscale = jnp.concatenate([jnp.full((1, D_FOX), LOG2E * HEAD_DIM ** -0.5, F32),
                              jnp.ones((1, 2 * D_FOX), F32)], axis=1)
    ones_hg = jnp.ones((1, 4 * D_HGRN), F32)
    w_in_t = jnp.swapaxes(w_in, 1, 2)
    w_f = jnp.pad(w_in[:, :, off_f:off_hg], ((0, 0), (0, 0), (0, LANES - HEADS))).astype(BF16)
    b_f = jnp.pad(b_fox_f, ((0, 0), (0, LANES - HEADS)))
    conv_b3 = conv_b.reshape(depth, 1, -1)
    w_out16, w_down16 = w_out.astype(BF16), w_down.astype(BF16)
    w_gate16, w_pproj16 = w_ple_gate.astype(BF16), w_ple_proj.astype(BF16)

    h = x.reshape(t, d)
    hn = None
    for i in range(depth):
        if hn is None:
            c3, hn = _fgate(h, w_f[i], b_f[i:i + 1], batch, seq, g_pre=g_mix_pre[i])
        else:
            c3 = _fgate(hn, w_f[i], b_f[i:i + 1], batch, seq)
        qkv = _proj(hn, w_in_t, i, 0, 3 * D_FOX, qscale, BF16, name="inproj_qkv")
        zh = _proj(hn, w_in_t, i, off_hg, 4 * D_HGRN, ones_hg, F32, name="inproj_hgrn")
        y_fox = _attention(qkv.reshape(batch, seq, 3 * D_FOX), c3, batch, seq)
        y_hg = _hgrn(zh, w_hgrn_lb, g_hgrn_out[i], i, batch, seq)

        h, hn = _outproj(y_fox, y_hg, w_out16, i, h, g_mix_post[i], g_ffn_pre[i])
        a = _upproj(hn, w_up, conv_w, conv_b3, i, seq)
        h, hn = _downproj(a, w_down16, i, h, g_ffn_post[i], g_ple_in[i])
        g_next = g_mix_pre[i + 1] if i + 1 < depth else None
        h, hn = _ple(p[i].reshape(t, -1), hn, w_pproj16, w_gate16, i, h, g_ple_post[i], g_next)
    return h.reshape(batch, seq, d)
```

```python
import functools
import math

import jax
import jax.numpy as jnp
from jax import lax
from jax.experimental import pallas as pl
from jax.experimental.pallas import tpu as pltpu

F32 = jnp.float32
BF16 = jnp.bfloat16

D_MODEL = 2048
D_FOX = 1024
D_HGRN = 1024
HEADS = 8
HEAD_DIM = 128
D_FF = 5632
CONV_WIDTH = 3
EPS = 1e-6
LOG2E = math.log2(math.e)
LANES = 128
SUBLANES = 8
HALO = SUBLANES
DEN_ROWS = SUBLANES

HGRN_CHUNK = 64
HGRN_CHUNKS_PER_STEP = 8
HGRN_LOOKAHEAD = 1
ATT_BLOCK = 512
ATT_GROUP = 4
ATT_LOOKAHEAD = 2
ROW_SUBTILES = 4

V7X_VMEM_BYTES = 64 * 1024 * 1024
VMEM_LIMIT = V7X_VMEM_BYTES - 8 * 1024 * 1024


def _params(*semantics):
    return pltpu.CompilerParams(dimension_semantics=semantics,
                                vmem_limit_bytes=VMEM_LIMIT)


def _rms(x, g):
    ms = jnp.mean(x * x, axis=-1, keepdims=True)
    return (x * lax.rsqrt(ms + EPS)) * g


def _log_sigmoid(x):
    return jnp.minimum(x, 0.0) - jnp.log(1.0 + jnp.exp(-jnp.abs(x)))


def _resident(shape, index_map):
    return pl.BlockSpec(shape, index_map, pipeline_mode=pl.Buffered(1))


def _proj_kernel(x_ref, w_ref, wtail_ref, s_ref, o_ref, wb_ref, *, shift):
    @pl.when(pl.program_id(1) == 0)
    def _():
        if shift:
            w = jnp.concatenate([w_ref[...], wtail_ref[...]], axis=0)
            wb_ref[...] = w[shift:shift + wb_ref.shape[0], :].astype(BF16)
        else:
            wb_ref[...] = w_ref[...].astype(BF16)

    acc = lax.dot_general(x_ref[...], wb_ref[...], (((1,), (1,)), ((), ())),
                          preferred_element_type=F32)
    o_ref[...] = (acc * s_ref[...]).astype(o_ref.dtype)


def _proj(x, w_t, layer, row0, n, colscale, out_dtype, tm=1024, tn=1024, name="proj"):
    t, k = x.shape
    shift = row0 % tn
    base = row0 - shift
    assert shift in (0, SUBLANES) and n % tn == 0 and tn % SUBLANES == 0
    tail = lambda j, i: (layer, (base + (j + 1) * tn) // SUBLANES if shift else 0, 0)
    return pl.pallas_call(
        functools.partial(_proj_kernel, shift=shift),
        grid=(n // tn, t // tm),
        in_specs=[pl.BlockSpec((tm, k), lambda j, i: (i, 0)),
                  pl.BlockSpec((None, tn, k), lambda j, i: (layer, base // tn + j, 0)),
                  pl.BlockSpec((None, SUBLANES, k), tail),
                  pl.BlockSpec((1, tn), lambda j, i: (0, j))],
        out_specs=pl.BlockSpec((tm, tn), lambda j, i: (i, j)),
        out_shape=jax.ShapeDtypeStruct((t, n), out_dtype),
        scratch_shapes=[pltpu.VMEM((tn, k), BF16)],
        compiler_params=_params("arbitrary", "arbitrary"),
        name=name,
    )(x, w_t, w_t, colscale)


def _fgate_kernel(*refs, tm, normalize):
    if normalize:
        x_ref, g_ref, w_ref, b_ref, o_ref, xn_ref, carry_ref = refs
        xn = _rms(x_ref[...], g_ref[...]).astype(BF16)
        xn_ref[...] = xn
    else:
        x_ref, w_ref, b_ref, o_ref, carry_ref = refs
        xn = x_ref[...]

    @pl.when(pl.program_id(1) == 0)
    def _():
        carry_ref[...] = jnp.zeros_like(carry_ref)

    f = jnp.dot(xn, w_ref[...], preferred_element_type=F32) + b_ref[...]
    c = _log_sigmoid(f)
    row = lax.broadcasted_iota(jnp.int32, c.shape, 0)
    d = 1
    while d < tm:
        c = c + jnp.where(row >= d, pltpu.roll(c, d, 0), 0.0)
        d *= 2
    c = c + carry_ref[0:1, :]
    carry_ref[...] = jnp.broadcast_to(c[tm - 1:tm, :], carry_ref.shape)

    x = c * (-LOG2E)
    hi = x.astype(BF16).astype(F32)
    rest = x - hi
    mid = rest.astype(BF16).astype(F32)
    lo = rest - mid
    lane = lax.broadcasted_iota(jnp.int32, c.shape, 1)
    parts = jnp.where(lane < HEADS, hi,
                      jnp.where(lane < 2 * HEADS, pltpu.roll(mid, HEADS, 1),
                                jnp.where(lane < 3 * HEADS, pltpu.roll(lo, 2 * HEADS, 1), 0.0)))
    o_ref[0] = parts.astype(BF16)


def _fgate(x, w_f, b_f, batch, seq, g_pre=None, tm=512):
    t, k = x.shape
    nt = seq // tm
    normalize = g_pre is not None
    rows = pl.BlockSpec((tm, k), lambda b, s: (b * nt + s, 0))
    in_specs = [rows] + ([pl.BlockSpec((1, k), lambda b, s: (0, 0))] if normalize else []) + [
        pl.BlockSpec((k, LANES), lambda b, s: (0, 0)),
        pl.BlockSpec((1, LANES), lambda b, s: (0, 0))]
    out_specs = [pl.BlockSpec((1, tm, LANES), lambda b, s: (b, s, 0))]
    out_shape = [jax.ShapeDtypeStruct((batch, seq, LANES), BF16)]
    if normalize:
        out_specs.append(rows)
        out_shape.append(jax.ShapeDtypeStruct((t, k), BF16))
    args = (x, g_pre.reshape(1, k), w_f, b_f) if normalize else (x, w_f, b_f)
    outs = pl.pallas_call(
        functools.partial(_fgate_kernel, tm=tm, normalize=normalize),
        grid=(batch, nt),
        in_specs=in_specs,
        out_specs=out_specs,
        out_shape=out_shape,
        scratch_shapes=[pltpu.VMEM((SUBLANES, LANES), F32)],
        compiler_params=_params("arbitrary", "arbitrary"),
        name="fgate",
    )(*args)
    return tuple(outs) if normalize else outs[0]


def _attn_kernel(qa_ref, qb_ref, k_ref, v_ref, c_ref, o_ref, vt_ref, q_scr, m_ref, acc_ref,
                 *, blk, nb, group):
    hg, ip = pl.program_id(1), pl.program_id(2)
    head_cols = [slice(hh * HEAD_DIM, (hh + 1) * HEAD_DIM) for hh in range(group)]

    @pl.when(ip == 0)
    def _():
        for hh in range(group):
            for r in range(nb):
                vblk = v_ref[0, r * blk:(r + 1) * blk, head_cols[hh]]
                vt_ref[hh * nb + r, 0:HEAD_DIM, :] = vblk.astype(F32).T.astype(BF16)
                vt_ref[hh * nb + r, HEAD_DIM:HEAD_DIM + DEN_ROWS, :] = jnp.ones((DEN_ROWS, blk),
                                                                                BF16)

    m_ref[...] = jnp.full(m_ref.shape, -jnp.inf, F32)
    acc_ref[...] = jnp.zeros(acc_ref.shape, F32)

    lane = lax.broadcasted_iota(jnp.int32, (blk, HEAD_DIM), 1)
    for hh in range(group):
        head = hg * group + hh
        gate_lanes = (lane == head) | (lane == head + HEADS) | (lane == head + 2 * HEADS)
        ones = jnp.where(gate_lanes, 1.0, 0.0).astype(BF16)
        for tile, q_ref in enumerate((qa_ref, qb_ref)):
            q_scr[group * tile + hh] = jnp.concatenate([q_ref[0, :, head_cols[hh]], ones], axis=1)

    def item(t):
        if t == nb:
            return 1, nb - 1 - ip, True
        if t >= nb // 2:
            return 1, t - ip - 1, False
        in_b = t > ip
        return in_b.astype(jnp.int32), jnp.where(in_b, t - ip - 1, t), t == ip

    def scores(t, heads):
        tile, kb, diag = item(t)
        ks = pl.multiple_of(kb * blk, blk)
        gate = c_ref[0, pl.ds(ks, blk), :]
        out = {}
        for hh in heads:
            k_aug = jnp.concatenate([k_ref[0, pl.ds(ks, blk), head_cols[hh]], gate], axis=1)
            s = lax.dot_general(k_aug, q_scr[group * tile + hh], (((1,), (1,)), ((), ())),
                                preferred_element_type=F32)
            if diag is not False:
                key = lax.broadcasted_iota(jnp.int32, s.shape, 0)
                qry = lax.broadcasted_iota(jnp.int32, s.shape, 1)
                keep = (key <= qry) if diag is True else ((key <= qry) | (t != ip))
                s = jnp.where(keep, s, -jnp.inf)
            out[hh] = (s, jnp.max(s, axis=0, keepdims=True))
        return out

    def softmax_pv(t, scored):
        tile, kb, _ = item(t)
        for hh in scored:
            idx = group * tile + hh
            s, bmax = scored[hh]
            m_old = m_ref[idx]
            m_new = jnp.maximum(m_old, bmax)
            alpha = jnp.exp2(m_old - m_new)
            p = jnp.exp2(s - m_new)
            pv = jnp.dot(vt_ref[hh * nb + kb], p.astype(BF16), preferred_element_type=F32)
            acc_ref[idx] = alpha * acc_ref[idx] + pv
            m_ref[idx] = m_new

    pending = [scores(t, range(group)) for t in range(ATT_LOOKAHEAD)]
    for t in range(nb + 1):
        if t + ATT_LOOKAHEAD < nb + 1:
            pending.append(scores(t + ATT_LOOKAHEAD, range(group)))
        softmax_pv(t, pending.pop(0))

    for tile in range(2):
        for hh in range(group):
            acc = acc_ref[group * tile + hh]
            o = (acc[0:HEAD_DIM] / acc[HEAD_DIM:HEAD_DIM + 1]).T
            o_ref[0, 0, tile, :, head_cols[hh]] = o.astype(o_ref.dtype)


def _attention(qkv, gate, batch, seq, blk=ATT_BLOCK, group=ATT_GROUP):
    nb = seq // blk
    width = group * HEAD_DIM
    ngroup = HEADS // group
    return pl.pallas_call(
        functools.partial(_attn_kernel, blk=blk, nb=nb, group=group),
        grid=(batch, ngroup, nb // 2),
        in_specs=[
            pl.BlockSpec((1, blk, width), lambda b, h, i: (b, i, h)),
            pl.BlockSpec((1, blk, width), lambda b, h, i: (b, nb - 1 - i, h)),
            pl.BlockSpec((1, seq, width), lambda b, h, i: (b, 0, ngroup + h)),
            pl.BlockSpec((1, seq, width), lambda b, h, i: (b, 0, 2 * ngroup + h)),
            pl.BlockSpec((1, seq, LANES), lambda b, h, i: (b, 0, 0)),
        ],
        out_specs=pl.BlockSpec((1, 1, 2, blk, width), lambda b, h, i: (b, i, 0, 0, h)),
        out_shape=jax.ShapeDtypeStruct((batch, nb // 2, 2, blk, D_FOX), BF16),
        scratch_shapes=[pltpu.VMEM((group * nb, HEAD_DIM + DEN_ROWS, blk), BF16),
                        pltpu.VMEM((2 * group, blk, 2 * HEAD_DIM), BF16),
                        pltpu.VMEM((2 * group, 1, blk), F32),
                        pltpu.VMEM((2 * group, HEAD_DIM + DEN_ROWS, blk), F32)],
        compiler_params=_params("arbitrary", "arbitrary", "arbitrary"),
        name="fox_attention",
    )(qkv, qkv, qkv, qkv, gate)


def _block_middle_rows(cum, cum_ref, chunk):
    row = lax.broadcasted_iota(jnp.int32, cum.shape, 0)
    spread = lambda r: jnp.broadcast_to(cum_ref[r:r + 1, :], (SUBLANES, cum.shape[-1]))
    out = []
    h = chunk // 2
    while h >= 4:
        reps = max(2 * h // SUBLANES, 1)
        r = jnp.concatenate([spread(b + h - 1) for b in range(0, chunk, 2 * h)
                             for _ in range(reps)], axis=0)
        out.append((h, r))
        h //= 2
    low = jnp.concatenate([spread(b + 1) for b in range(0, chunk, SUBLANES)], axis=0)
    high = jnp.concatenate([spread(b + 5) for b in range(0, chunk, SUBLANES)], axis=0)
    out.append((2, jnp.where(row % SUBLANES < 4, low, high)))
    tiles = cum.reshape(chunk // SUBLANES, SUBLANES, cum.shape[-1])
    above = pltpu.roll(tiles, 1, 1).reshape(cum.shape)
    out.append((1, jnp.where(row % 2 == 1, above, cum)))
    return out


def _hgrn_kernel(q_ref, f_ref, i_ref, g_ref, wlb_ref, go_ref, o_ref, state_ref, cum_ref,
                 *, chunk, layer):
    @pl.when(pl.program_id(1) == 0)
    def _():
        state_ref[...] = jnp.zeros_like(state_ref)

    if layer > 0:
        w = wlb_ref[...]
        ew = jnp.exp(w - jnp.max(w, axis=0, keepdims=True))
        sm = ew / jnp.sum(ew, axis=0, keepdims=True)
        lb_all = sm[1:2]
        for dpt in range(2, layer + 1):
            lb_all = lb_all + sm[dpt:dpt + 1]
        log_lb_all = jnp.log(lb_all)
        log_1mlb_all = jnp.log1p(-lb_all)

    row = lax.broadcasted_iota(jnp.int32, (chunk, HEAD_DIM), 0)
    tile_row = lax.broadcasted_iota(jnp.int32, (chunk // SUBLANES, SUBLANES, HEAD_DIM), 1)
    rr = lax.broadcasted_iota(jnp.int32, (chunk, chunk), 0)
    cc = lax.broadcasted_iota(jnp.int32, (chunk, chunk), 1)
    pair_mask = {}
    side = {}
    h = chunk // 2
    while h >= 1:
        pair_mask[h] = (((rr // (2 * h)) == (cc // (2 * h)))
                        & (rr % (2 * h) >= h) & (cc % (2 * h) < h))
        side[h] = jnp.where(row % (2 * h) >= h, 1.0, -1.0)
        h //= 2

    def scores(rows):
        stage = []
        for hd in range(HEADS):
            sl = slice(hd * HEAD_DIM, (hd + 1) * HEAD_DIM)
            q = q_ref[rows, sl]
            f = f_ref[rows, sl]
            v = i_ref[rows, sl]
            if layer == 0:
                log_f = _log_sigmoid(f)
                k = 1.0 / (1.0 + jnp.exp(f))
            else:
                a = log_lb_all[:, sl]
                b = log_1mlb_all[:, sl] + _log_sigmoid(f)
                log_f = jnp.maximum(a, b) + jnp.log(1.0 + jnp.exp(-jnp.abs(a - b)))
                k = (1.0 - lb_all[:, sl]) * (1.0 / (1.0 + jnp.exp(f)))

            cum = log_f * LOG2E
            tiles = cum.reshape(chunk // SUBLANES, SUBLANES, HEAD_DIM)
            d = 1
            while d < SUBLANES:
                tiles = tiles + jnp.where(tile_row >= d, pltpu.roll(tiles, d, 1), 0.0)
                d *= 2
            offset, offsets = None, [jnp.zeros((1, 1, HEAD_DIM), F32)]
            for tile in range(chunk // SUBLANES - 1):
                total = tiles[tile:tile + 1, SUBLANES - 1:SUBLANES, :]
                offset = total if offset is None else offset + total
                offsets.append(offset)
            cum = (tiles + jnp.concatenate(offsets, axis=0)).reshape(chunk, HEAD_DIM)
            last = cum[chunk - 1:chunk, :]

            sc = jnp.zeros((chunk, chunk), F32)
            q16, k16 = q.astype(BF16), k.astype(BF16)
            cum_ref[hd] = cum
            for h, r in _block_middle_rows(cum, cum_ref.at[hd], chunk):
                decay = jnp.exp2((cum - r) * side[h]).astype(BF16)
                s_l = lax.dot_general(q16 * decay, k16 * decay,
                                      (((1,), (1,)), ((), ())), preferred_element_type=F32)
                sc = jnp.where(pair_mask[h], s_l, sc)
            stage.append((sc.astype(BF16), v.astype(BF16),
                          jnp.sum(q * k, axis=-1, keepdims=True) * v,
                          (q * jnp.exp2(cum)).astype(BF16),
                          (k * jnp.exp2(last - cum)).astype(BF16), jnp.exp2(last)))
        return stage

    def recur(rows, stage):
        for hd in range(HEADS):
            sl = slice(hd * HEAD_DIM, (hd + 1) * HEAD_DIM)
            sc16, v16, o_diag, qd, kd, decay_last = stage[hd]
            st = state_ref[hd]
            o = o_diag + jnp.dot(sc16, v16, preferred_element_type=F32)
            o = o + lax.dot_general(qd, st.astype(BF16), (((1,), (1,)), ((), ())),
                                    preferred_element_type=F32)
            upd = lax.dot_general(v16, kd, (((0,), (0,)), ((), ())),
                                  preferred_element_type=F32)
            state_ref[hd] = st * decay_last + upd

            g = g_ref[rows, sl]
            y = _rms(o, go_ref[...]) * (g * (1.0 / (1.0 + jnp.exp(-g))))
            o_ref[rows, sl] = y.astype(o_ref.dtype)

    n_chunks = q_ref.shape[0] // chunk
    rows = [slice(c * chunk, (c + 1) * chunk) for c in range(n_chunks)]
    pending = [scores(rows[c]) for c in range(min(HGRN_LOOKAHEAD, n_chunks))]
    for c in range(n_chunks):
        if c + HGRN_LOOKAHEAD < n_chunks:
            pending.append(scores(rows[c + HGRN_LOOKAHEAD]))
        recur(rows[c], pending.pop(0))


def _hgrn(zh, w_lb, g_o, layer, batch, seq, chunk=HGRN_CHUNK,
          chunks_per_step=HGRN_CHUNKS_PER_STEP):
    t = zh.shape[0]
    rows = chunk * chunks_per_step
    ns = seq // rows
    spec = lambda grp: pl.BlockSpec((rows, D_HGRN), lambda b, c: (b * ns + c, grp))
    return pl.pallas_call(
        functools.partial(_hgrn_kernel, chunk=chunk, layer=layer),
        grid=(batch, ns),
        in_specs=[spec(0), spec(1), spec(2), spec(3),
                  pl.BlockSpec(w_lb.shape, lambda b, c: (0, 0)),
                  pl.BlockSpec((1, HEAD_DIM), lambda b, c: (0, 0))],
        out_specs=pl.BlockSpec((rows, D_HGRN), lambda b, c: (b * ns + c, 0)),
        out_shape=jax.ShapeDtypeStruct((t, D_HGRN), BF16),
        scratch_shapes=[pltpu.VMEM((HEADS, HEAD_DIM, HEAD_DIM), F32),
                        pltpu.VMEM((HEADS, chunk, HEAD_DIM), F32)],
        compiler_params=_params("arbitrary", "arbitrary"),
        name="hgrn2",
    )(zh, zh, zh, zh, w_lb, g_o.reshape(1, HEAD_DIM))


def _residual_norm_store(branch, rows, h_ref, gpost_ref, gnext_ref, h_out_ref, hn_out_ref):
    h = h_ref[rows, :] + _rms(branch, gpost_ref[...])
    h_out_ref[rows, :] = h
    if hn_out_ref is not None:
        hn_out_ref[rows, :] = _rms(h, gnext_ref[...]).astype(hn_out_ref.dtype)


def _subtile_rows(tm):
    sub = tm // ROW_SUBTILES
    return [slice(s * sub, (s + 1) * sub) for s in range(ROW_SUBTILES)]


def _outproj_kernel(ya_ref, yb_ref, wa_ref, wb_ref, h_ref, gpost_ref, gnext_ref,
                    h_out_ref, hn_out_ref):
    mixes = []
    for rows in _subtile_rows(h_ref.shape[0]):
        mix = jnp.dot(ya_ref[rows, :], wa_ref[...], preferred_element_type=F32)
        mixes.append(mix + jnp.dot(yb_ref[rows, :], wb_ref[...], preferred_element_type=F32))
    for rows, mix in zip(_subtile_rows(h_ref.shape[0]), mixes):
        _residual_norm_store(mix, rows, h_ref, gpost_ref, gnext_ref, h_out_ref, hn_out_ref)


def _outproj(ya, yb, w_out, layer, h, g_post, g_next):
    t, d = h.shape
    _, half_nb, _, tm, ka = ya.shape
    nb = 2 * half_nb
    kb = yb.shape[1]
    row = lambda width: pl.BlockSpec((tm, width), lambda i: (i, 0))
    gain = _resident((1, d), lambda i: (0, 0))

    def ya_index(i):
        b, tile = i // nb, i % nb
        second = tile >= half_nb
        return (b, jnp.where(second, nb - 1 - tile, tile), second.astype(jnp.int32), 0, 0)

    return pl.pallas_call(
        _outproj_kernel,
        grid=(t // tm,),
        in_specs=[pl.BlockSpec((None, None, None, tm, ka), ya_index), row(kb),
                  _resident((None, ka, d), lambda i: (layer, 0, 0)),
                  _resident((None, kb, d), lambda i: (layer, ka // kb, 0)),
                  row(d), gain, gain],
        out_specs=[row(d), row(d)],
        out_shape=[jax.ShapeDtypeStruct((t, d), F32), jax.ShapeDtypeStruct((t, d), BF16)],
        compiler_params=_params("arbitrary"),
        name="outproj",
    )(ya, yb, w_out, w_out, h, g_post.reshape(1, d), g_next.reshape(1, d))


def _upproj_kernel(x_ref, wg_ref, wu_ref, cwg_ref, cwu_ref, cbg_ref, cbu_ref, o_ref,
                   wgb_ref, wub_ref, ug_ref, uu_ref, *, tm, tiles_per_seq):
    i = pl.program_id(1)

    @pl.when(i == 0)
    def _():
        wgb_ref[...] = wg_ref[...].astype(BF16)
        wub_ref[...] = wu_ref[...].astype(BF16)

    @pl.when(i % tiles_per_seq == 0)
    def _():
        ug_ref[0:HALO, :] = jnp.zeros((HALO, ug_ref.shape[1]), F32)
        uu_ref[0:HALO, :] = jnp.zeros((HALO, uu_ref.shape[1]), F32)

    x = x_ref[...]
    ug_ref[HALO:tm + HALO, :] = jnp.dot(x, wgb_ref[...], preferred_element_type=F32)
    uu_ref[HALO:tm + HALO, :] = jnp.dot(x, wub_ref[...], preferred_element_type=F32)

    def conv(u_ref, cw_ref, cb_ref):
        acc = cb_ref[...]
        for j in reversed(range(CONV_WIDTH)):
            back = CONV_WIDTH - 1 - j
            acc = acc + cw_ref[j:j + 1, :] * u_ref[HALO - back:tm + HALO - back, :]
        return acc

    gate = conv(ug_ref, cwg_ref, cbg_ref)
    up = conv(uu_ref, cwu_ref, cbu_ref)
    o_ref[...] = (gate * (1.0 / (1.0 + jnp.exp(-gate))) * up).astype(o_ref.dtype)

    ug_ref[0:HALO, :] = ug_ref[tm:tm + HALO, :]
    uu_ref[0:HALO, :] = uu_ref[tm:tm + HALO, :]


def _upproj(xn, w_up, conv_w, conv_b, layer, seq, tm=1024, tn=512):
    t, k = xn.shape
    nj = D_FF // tn
    lay = lambda r, off: pl.BlockSpec((None, r, tn), lambda j, i: (layer, 0, j + off))
    return pl.pallas_call(
        functools.partial(_upproj_kernel, tm=tm, tiles_per_seq=seq // tm),
        grid=(nj, t // tm),
        in_specs=[pl.BlockSpec((tm, k), lambda j, i: (i, 0)),
                  lay(k, 0), lay(k, nj),
                  lay(CONV_WIDTH, 0), lay(CONV_WIDTH, nj),
                  lay(1, 0), lay(1, nj)],
        out_specs=pl.BlockSpec((tm, tn), lambda j, i: (i, j)),
        out_shape=jax.ShapeDtypeStruct((t, D_FF), BF16),
        scratch_shapes=[pltpu.VMEM((k, tn), BF16), pltpu.VMEM((k, tn), BF16),
                        pltpu.VMEM((tm + HALO, tn), F32), pltpu.VMEM((tm + HALO, tn), F32)],
        compiler_params=_params("arbitrary", "arbitrary"),
        name="upproj_conv",
    )(xn, w_up, w_up, conv_w, conv_w, conv_b, conv_b)


def _downproj_kernel(a_ref, w_ref, h_ref, gpost_ref, gnext_ref, h_out_ref, hn_out_ref):
    ff = jnp.dot(a_ref[...], w_ref[...], preferred_element_type=F32)
    _residual_norm_store(ff, slice(None), h_ref, gpost_ref, gnext_ref, h_out_ref, hn_out_ref)


def _downproj(a, w_down, layer, h, g_post, g_next, tm=256):
    t, d = h.shape
    k = a.shape[1]
    row = lambda width: pl.BlockSpec((tm, width), lambda i: (i, 0))
    gain = _resident((1, d), lambda i: (0, 0))
    return pl.pallas_call(
        _downproj_kernel,
        grid=(t // tm,),
        in_specs=[row(k), _resident((None, k, d), lambda i: (layer, 0, 0)), row(d), gain, gain],
        out_specs=[row(d), row(d)],
        out_shape=[jax.ShapeDtypeStruct((t, d), F32), jax.ShapeDtypeStruct((t, d), BF16)],
        compiler_params=_params("arbitrary"),
        name="downproj",
    )(a, w_down, h, g_post.reshape(1, d), g_next.reshape(1, d))


def _ple_kernel(p_ref, hn_ref, wp_ref, wg_ref, h_ref, gpost_ref, gnext_ref,
                h_out_ref, hn_out_ref=None):
    pairs = []
    for rows in _subtile_rows(h_ref.shape[0]):
        e = jnp.dot(p_ref[rows, :].astype(BF16), wp_ref[...], preferred_element_type=F32)
        z = jnp.dot(hn_ref[rows, :], wg_ref[...], preferred_element_type=F32)
        pairs.append((e, z))
    for rows, (e, z) in zip(_subtile_rows(h_ref.shape[0]), pairs):
        gated = e * (1.0 / (1.0 + jnp.exp(-z)))
        _residual_norm_store(gated, rows, h_ref, gpost_ref, gnext_ref, h_out_ref, hn_out_ref)


def _ple(p, hn, w_proj, w_gate, layer, h, g_post, g_next, tm=512):
    t, d = h.shape
    kp = p.shape[1]
    row = lambda width: pl.BlockSpec((tm, width), lambda i: (i, 0))
    gain = _resident((1, d), lambda i: (0, 0))
    last = g_next is None
    out_shape = [jax.ShapeDtypeStruct((t, d), F32)]
    if not last:
        out_shape.append(jax.ShapeDtypeStruct((t, d), BF16))
    outs = pl.pallas_call(
        _ple_kernel,
        grid=(t // tm,),
        in_specs=[row(kp), row(d),
                  _resident((None, kp, d), lambda i: (layer, 0, 0)),
                  _resident((None, d, d), lambda i: (layer, 0, 0)),
                  row(d), gain, gain],
        out_specs=[row(d)] * len(out_shape),
        out_shape=out_shape,
        compiler_params=_params("arbitrary"),
        name="ple",
    )(p, hn, w_proj, w_gate, h, g_post.reshape(1, d),
      (g_post if last else g_next).reshape(1, d))
    return (outs[0], None) if last else tuple(outs)


def kernel(x, p, g_mix_pre, w_in, b_fox_f, w_hgrn_lb, g_hgrn_out, w_out, g_mix_post,
           g_ffn_pre, w_up, conv_w, conv_b, w_down, g_ffn_post,
           g_ple_in, w_ple_gate, w_ple_proj, g_ple_post):
    batch, seq, d = x.shape
    depth = w_in.shape[0]
    t = batch * seq
    off_f = 3 * D_FOX
    off_hg = off_f + HEADS

    qscale = jnp.concatenate([jnp.full((1, D_FOX), LOG2E * HEAD_DIM ** -0.5, F32),
                              jnp.ones((1, 2 * D_FOX), F32)], axis=1)
    ones_hg = jnp.ones((1, 4 * D_HGRN), F32)
    w_in_t = jnp.swapaxes(w_in, 1, 2)
    w_f = jnp.pad(w_in[:, :, off_f:off_hg], ((0, 0), (0, 0), (0, LANES - HEADS))).astype(BF16)
    b_f = jnp.pad(b_fox_f, ((0, 0), (0, LANES - HEADS)))
    conv_b3 = conv_b.reshape(depth, 1, -1)
    w_out16, w_down16 = w_out.astype(BF16), w_down.astype(BF16)
    w_gate16, w_pproj16 = w_ple_gate.astype(BF16), w_ple_proj.astype(BF16)

    h = x.reshape(t, d)
    hn = None
    for i in range(depth):
        if hn is None:
            c3, hn = _fgate(h, w_f[i], b_f[i:i + 1], batch, seq, g_pre=g_mix_pre[i])
        else:
            c3 = _fgate(hn, w_f[i], b_f[i:i + 1], batch, seq)
        qkv = _proj(hn, w_in_t, i, 0, 3 * D_FOX, qscale, BF16, name="inproj_qkv")
        zh = _proj(hn, w_in_t, i, off_hg, 4 * D_HGRN, ones_hg, F32, name="inproj_hgrn")
        y_fox = _attention(qkv.reshape(batch, seq, 3 * D_FOX), c3, batch, seq)
        y_hg = _hgrn(zh, w_hgrn_lb, g_hgrn_out[i], i, batch, seq)

        h, hn = _outproj(y_fox, y_hg, w_out16, i, h, g_mix_post[i], g_ffn_pre[i])
        a = _upproj(hn, w_up, conv_w, conv_b3, i, seq)
        h, hn = _downproj(a, w_down16, i, h, g_ffn_post[i], g_ple_in[i])
        g_next = g_mix_pre[i + 1] if i + 1 < depth else None
        h, hn = _ple(p[i].reshape(t, -1), hn, w_pproj16, w_gate16, i, h, g_ple_post[i], g_next)
    return h.reshape(batch, seq, d)
```

```python
import functools
import math

import jax
import jax.numpy as jnp
from jax import lax
from jax.experimental import pallas as pl
from jax.experimental.pallas import tpu as pltpu

F32 = jnp.float32
BF16 = jnp.bfloat16

D_MODEL = 2048
D_FOX = 1024
D_HGRN = 1024
HEADS = 8
HEAD_DIM = 128
D_FF = 5632
CONV_WIDTH = 3
EPS = 1e-6
LOG2E = math.log2(math.e)
LANES = 128
SUBLANES = 8
HALO = SUBLANES
DEN_ROWS = SUBLANES

HGRN_CHUNK = 64
HGRN_CHUNKS_PER_STEP = 8
HGRN_LOOKAHEAD = 1
ATT_BLOCK = 512
ATT_GROUP = 4
ATT_LOOKAHEAD = 2
ROW_SUBTILES = 4

V7X_VMEM_BYTES = 64 * 1024 * 1024
VMEM_LIMIT = V7X_VMEM_BYTES - 8 * 1024 * 1024


def _params(*semantics):
    return pltpu.CompilerParams(dimension_semantics=semantics,
                                vmem_limit_bytes=VMEM_LIMIT)


def _rms(x, g):
    ms = jnp.mean(x * x, axis=-1, keepdims=True)
    return (x * lax.rsqrt(ms + EPS)) * g


def _log_sigmoid(x):
    return jnp.minimum(x, 0.0) - jnp.log(1.0 + jnp.exp(-jnp.abs(x)))


def _resident(shape, index_map):
    return pl.BlockSpec(shape, index_map, pipeline_mode=pl.Buffered(1))


def _proj_kernel(x_ref, w_ref, wtail_ref, s_ref, o_ref, wb_ref, *, shift):
    @pl.when(pl.program_id(1) == 0)
    def _():
        if shift:
            w = jnp.concatenate([w_ref[...], wtail_ref[...]], axis=0)
            wb_ref[...] = w[shift:shift + wb_ref.shape[0], :].astype(BF16)
        else:
            wb_ref[...] = w_ref[...].astype(BF16)

    acc = lax.dot_general(x_ref[...], wb_ref[...], (((1,), (1,)), ((), ())),
                          preferred_element_type=F32)
    o_ref[...] = (acc * s_ref[...]).astype(o_ref.dtype)


def _proj(x, w_t, layer, row0, n, colscale, out_dtype, tm=1024, tn=1024, name="proj"):
    t, k = x.shape
    shift = row0 % tn
    base = row0 - shift
    assert shift in (0, SUBLANES) and n % tn == 0 and tn % SUBLANES == 0
    tail = lambda j, i: (layer, (base + (j + 1) * tn) // SUBLANES if shift else 0, 0)
    return pl.pallas_call(
        functools.partial(_proj_kernel, shift=shift),
        grid=(n // tn, t // tm),
        in_specs=[pl.BlockSpec((tm, k), lambda j, i: (i, 0)),
                  pl.BlockSpec((None, tn, k), lambda j, i: (layer, base // tn + j, 0)),
                  pl.BlockSpec((None, SUBLANES, k), tail),
                  pl.BlockSpec((1, tn), lambda j, i: (0, j))],
        out_specs=pl.BlockSpec((tm, tn), lambda j, i: (i, j)),
        out_shape=jax.ShapeDtypeStruct((t, n), out_dtype),
        scratch_shapes=[pltpu.VMEM((tn, k), BF16)],
        compiler_params=_params("arbitrary", "arbitrary"),
        name=name,
    )(x, w_t, w_t, colscale)


def _fgate_kernel(*refs, tm, normalize):
    if normalize:
        x_ref, g_ref, w_ref, b_ref, o_ref, xn_ref, carry_ref = refs
        xn = _rms(x_ref[...], g_ref[...]).astype(BF16)
        xn_ref[...] = xn
    else:
        x_ref, w_ref, b_ref, o_ref, carry_ref = refs
        xn = x_ref[...]

    @pl.when(pl.program_id(1) == 0)
    def _():
        carry_ref[...] = jnp.zeros_like(carry_ref)

    f = jnp.dot(xn, w_ref[...], preferred_element_type=F32) + b_ref[...]
    c = _log_sigmoid(f)
    row = lax.broadcasted_iota(jnp.int32, c.shape, 0)
    d = 1
    while d < tm:
        c = c + jnp.where(row >= d, pltpu.roll(c, d, 0), 0.0)
        d *= 2
    c = c + carry_ref[0:1, :]
    carry_ref[...] = jnp.broadcast_to(c[tm - 1:tm, :], carry_ref.shape)

    x = c * (-LOG2E)
    hi = x.astype(BF16).astype(F32)
    rest = x - hi
    mid = rest.astype(BF16).astype(F32)
    lo = rest - mid
    lane = lax.broadcasted_iota(jnp.int32, c.shape, 1)
    parts = jnp.where(lane < HEADS, hi,
                      jnp.where(lane < 2 * HEADS, pltpu.roll(mid, HEADS, 1),
                                jnp.where(lane < 3 * HEADS, pltpu.roll(lo, 2 * HEADS, 1), 0.0)))
    o_ref[0] = parts.astype(BF16)


def _fgate(x, w_f, b_f, batch, seq, g_pre=None, tm=512):
    t, k = x.shape
    nt = seq // tm
    normalize = g_pre is not None
    rows = pl.BlockSpec((tm, k), lambda b, s: (b * nt + s, 0))
    in_specs = [rows] + ([pl.BlockSpec((1, k), lambda b, s: (0, 0))] if normalize else []) + [
        pl.BlockSpec((k, LANES), lambda b, s: (0, 0)),
        pl.BlockSpec((1, LANES), lambda b, s: (0, 0))]
    out_specs = [pl.BlockSpec((1, tm, LANES), lambda b, s: (b, s, 0))]
    out_shape = [jax.ShapeDtypeStruct((batch, seq, LANES), BF16)]
    if normalize:
        out_specs.append(rows)
        out_shape.append(jax.ShapeDtypeStruct((t, k), BF16))
    args = (x, g_pre.reshape(1, k), w_f, b_f) if normalize else (x, w_f, b_f)
    outs = pl.pallas_call(
        functools.partial(_fgate_kernel, tm=tm, normalize=normalize),
        grid=(batch, nt),
        in_specs=in_specs,
        out_specs=out_specs,
        out_shape=out_shape,
        scratch_shapes=[pltpu.VMEM((SUBLANES, LANES), F32)],
        compiler_params=_params("arbitrary", "arbitrary"),
        name="fgate",
    )(*args)
    return tuple(outs) if normalize else outs[0]


def _attn_kernel(qa_ref, qb_ref, k_ref, v_ref, c_ref, o_ref, vt_ref, q_scr, m_ref, acc_ref,
                 *, blk, nb, group):
    hg, ip = pl.program_id(1), pl.program_id(2)
    head_cols = [slice(hh * HEAD_DIM, (hh + 1) * HEAD_DIM) for hh in range(group)]

    @pl.when(ip == 0)
    def _():
        for hh in range(group):
            for r in range(nb):
                vblk = v_ref[0, r * blk:(r + 1) * blk, head_cols[hh]]
                vt_ref[hh * nb + r, 0:HEAD_DIM, :] = vblk.astype(F32).T.astype(BF16)
                vt_ref[hh * nb + r, HEAD_DIM:HEAD_DIM + DEN_ROWS, :] = jnp.ones((DEN_ROWS, blk),
                                                                                BF16)

    m_ref[...] = jnp.full(m_ref.shape, -jnp.inf, F32)
    acc_ref[...] = jnp.zeros(acc_ref.shape, F32)

    lane = lax.broadcasted_iota(jnp.int32, (blk, HEAD_DIM), 1)
    for hh in range(group):
        head = hg * group + hh
        gate_lanes = (lane == head) | (lane == head + HEADS) | (lane == head + 2 * HEADS)
        ones = jnp.where(gate_lanes, 1.0, 0.0).astype(BF16)
        for tile, q_ref in enumerate((qa_ref, qb_ref)):
            q_scr[group * tile + hh] = jnp.concatenate([q_ref[0, :, head_cols[hh]], ones], axis=1)

    def item(t):
        if t == 0:
            return 0, ip, True
        if t == nb:
            return 1, nb - 1 - ip, True
        if t >= nb // 2:
            return 1, t - 1 - ip, False
        in_b = t > ip
        return in_b.astype(jnp.int32), jnp.where(in_b, t - 1 - ip, t - 1), False

    def scores(t, heads):
        tile, kb, diag = item(t)
        ks = pl.multiple_of(kb * blk, blk)
        gate = c_ref[0, pl.ds(ks, blk), :]
        out = {}
        for hh in heads:
            k_aug = jnp.concatenate([k_ref[0, pl.ds(ks, blk), head_cols[hh]], gate], axis=1)
            s = lax.dot_general(k_aug, q_scr[group * tile + hh], (((1,), (1,)), ((), ())),
                                preferred_element_type=F32)
            if diag:
                key = lax.broadcasted_iota(jnp.int32, s.shape, 0)
                qry = lax.broadcasted_iota(jnp.int32, s.shape, 1)
                s = jnp.where(key <= qry, s, -jnp.inf)
            out[hh] = (s, jnp.max(s, axis=0, keepdims=True))
        return out

    def softmax_pv(t, scored):
        tile, kb, _ = item(t)
        for hh in scored:
            idx = group * tile + hh
            s, bmax = scored[hh]
            m_old = m_ref[idx]
            m_new = jnp.maximum(m_old, bmax)
            alpha = jnp.exp2(m_old - m_new)
            p = jnp.exp2(s - m_new)
            pv = jnp.dot(vt_ref[hh * nb + kb], p.astype(BF16), preferred_element_type=F32)
            acc_ref[idx] = alpha * acc_ref[idx] + pv
            m_ref[idx] = m_new

    pending = [scores(t, range(group)) for t in range(ATT_LOOKAHEAD)]
    for t in range(nb + 1):
        if t + ATT_LOOKAHEAD < nb + 1:
            pending.append(scores(t + ATT_LOOKAHEAD, range(group)))
        softmax_pv(t, pending.pop(0))

    for tile in range(2):
        for hh in range(group):
            acc = acc_ref[group * tile + hh]
            o = (acc[0:HEAD_DIM] / acc[HEAD_DIM:HEAD_DIM + 1]).T
            o_ref[0, 0, tile, :, head_cols[hh]] = o.astype(o_ref.dtype)


def _attention(qkv, gate, batch, seq, blk=ATT_BLOCK, group=ATT_GROUP):
    nb = seq // blk
    width = group * HEAD_DIM
    ngroup = HEADS // group
    return pl.pallas_call(
        functools.partial(_attn_kernel, blk=blk, nb=nb, group=group),
        grid=(batch, ngroup, nb // 2),
        in_specs=[
            pl.BlockSpec((1, blk, width), lambda b, h, i: (b, i, h)),
            pl.BlockSpec((1, blk, width), lambda b, h, i: (b, nb - 1 - i, h)),
            pl.BlockSpec((1, seq, width), lambda b, h, i: (b, 0, ngroup + h)),
            pl.BlockSpec((1, seq, width), lambda b, h, i: (b, 0, 2 * ngroup + h)),
            pl.BlockSpec((1, seq, LANES), lambda b, h, i: (b, 0, 0)),
        ],
        out_specs=pl.BlockSpec((1, 1, 2, blk, width), lambda b, h, i: (b, i, 0, 0, h)),
        out_shape=jax.ShapeDtypeStruct((batch, nb // 2, 2, blk, D_FOX), BF16),
        scratch_shapes=[pltpu.VMEM((group * nb, HEAD_DIM + DEN_ROWS, blk), BF16),
                        pltpu.VMEM((2 * group, blk, 2 * HEAD_DIM), BF16),
                        pltpu.VMEM((2 * group, 1, blk), F32),
                        pltpu.VMEM((2 * group, HEAD_DIM + DEN_ROWS, blk), F32)],
        compiler_params=_params("arbitrary", "arbitrary", "arbitrary"),
        name="fox_attention",
    )(qkv, qkv, qkv, qkv, gate)


def _block_middle_rows(cum, cum_ref, chunk):
    row = lax.broadcasted_iota(jnp.int32, cum.shape, 0)
    spread = lambda r: jnp.broadcast_to(cum_ref[r:r + 1, :], (SUBLANES, cum.shape[-1]))
    out = []
    h = chunk // 2
    while h >= 4:
        reps = max(2 * h // SUBLANES, 1)
        r = jnp.concatenate([spread(b + h - 1) for b in range(0, chunk, 2 * h)
                             for _ in range(reps)], axis=0)
        out.append((h, r))
        h //= 2
    low = jnp.concatenate([spread(b + 1) for b in range(0, chunk, SUBLANES)], axis=0)
    high = jnp.concatenate([spread(b + 5) for b in range(0, chunk, SUBLANES)], axis=0)
    out.append((2, jnp.where(row % SUBLANES < 4, low, high)))
    tiles = cum.reshape(chunk // SUBLANES, SUBLANES, cum.shape[-1])
    above = pltpu.roll(tiles, 1, 1).reshape(cum.shape)
    out.append((1, jnp.where(row % 2 == 1, above, cum)))
    return out


def _hgrn_kernel(q_ref, f_ref, i_ref, g_ref, wlb_ref, go_ref, o_ref, state_ref, cum_ref,
                 *, chunk, layer):
    @pl.when(pl.program_id(1) == 0)
    def _():
        state_ref[...] = jnp.zeros_like(state_ref)

    if layer > 0:
        w = wlb_ref[...]
        ew = jnp.exp(w - jnp.max(w, axis=0, keepdims=True))
        sm = ew / jnp.sum(ew, axis=0, keepdims=True)
        lb_all = sm[1:2]
        for dpt in range(2, layer + 1):
            lb_all = lb_all + sm[dpt:dpt + 1]
        log_lb_all = jnp.log(lb_all)
        log_1mlb_all = jnp.log1p(-lb_all)

    row = lax.broadcasted_iota(jnp.int32, (chunk, HEAD_DIM), 0)
    tile_row = lax.broadcasted_iota(jnp.int32, (chunk // SUBLANES, SUBLANES, HEAD_DIM), 1)
    rr = lax.broadcasted_iota(jnp.int32, (chunk, chunk), 0)
    cc = lax.broadcasted_iota(jnp.int32, (chunk, chunk), 1)
    pair_mask = {}
    side = {}
    h = chunk // 2
    while h >= 1:
        pair_mask[h] = (((rr // (2 * h)) == (cc // (2 * h)))
                        & (rr % (2 * h) >= h) & (cc % (2 * h) < h))
        side[h] = jnp.where(row % (2 * h) >= h, 1.0, -1.0)
        h //= 2

    def scores(rows):
        stage = []
        for hd in range(HEADS):
            sl = slice(hd * HEAD_DIM, (hd + 1) * HEAD_DIM)
            q = q_ref[rows, sl]
            f = f_ref[rows, sl]
            v = i_ref[rows, sl]
            if layer == 0:
                log_f = _log_sigmoid(f)
                k = 1.0 / (1.0 + jnp.exp(f))
            else:
                a = log_lb_all[:, sl]
                b = log_1mlb_all[:, sl] + _log_sigmoid(f)
                log_f = jnp.maximum(a, b) + jnp.log(1.0 + jnp.exp(-jnp.abs(a - b)))
                k = (1.0 - lb_all[:, sl]) * (1.0 / (1.0 + jnp.exp(f)))

            cum = log_f * LOG2E
            tiles = cum.reshape(chunk // SUBLANES, SUBLANES, HEAD_DIM)
            d = 1
            while d < SUBLANES:
                tiles = tiles + jnp.where(tile_row >= d, pltpu.roll(tiles, d, 1), 0.0)
                d *= 2
            offset, offsets = None, [jnp.zeros((1, 1, HEAD_DIM), F32)]
            for tile in range(chunk // SUBLANES - 1):
                total = tiles[tile:tile + 1, SUBLANES - 1:SUBLANES, :]
                offset = total if offset is None else offset + total
                offsets.append(offset)
            cum = (tiles + jnp.concatenate(offsets, axis=0)).reshape(chunk, HEAD_DIM)
            last = cum[chunk - 1:chunk, :]

            sc = jnp.zeros((chunk, chunk), F32)
            q16, k16 = q.astype(BF16), k.astype(BF16)
            cum_ref[hd] = cum
            for h, r in _block_middle_rows(cum, cum_ref.at[hd], chunk):
                decay = jnp.exp2((cum - r) * side[h]).astype(BF16)
                s_l = lax.dot_general(q16 * decay, k16 * decay,
                                      (((1,), (1,)), ((), ())), preferred_element_type=F32)
                sc = jnp.where(pair_mask[h], s_l, sc)
            stage.append((sc.astype(BF16), v.astype(BF16),
                          jnp.sum(q * k, axis=-1, keepdims=True) * v,
                          (q * jnp.exp2(cum)).astype(BF16),
                          (k * jnp.exp2(last - cum)).astype(BF16), jnp.exp2(last)))
        return stage

    def recur(rows, stage):
        for hd in range(HEADS):
            sl = slice(hd * HEAD_DIM, (hd + 1) * HEAD_DIM)
            sc16, v16, o_diag, qd, kd, decay_last = stage[hd]
            st = state_ref[hd]
            o = o_diag + jnp.dot(sc16, v16, preferred_element_type=F32)
            o = o + lax.dot_general(qd, st.astype(BF16), (((1,), (1,)), ((), ())),
                                    preferred_element_type=F32)
            upd = lax.dot_general(v16, kd, (((0,), (0,)), ((), ())),
                                  preferred_element_type=F32)
            state_ref[hd] = st * decay_last + upd

            g = g_ref[rows, sl]
            y = _rms(o, go_ref[...]) * (g * (1.0 / (1.0 + jnp.exp(-g))))
            o_ref[rows, sl] = y.astype(o_ref.dtype)

    n_chunks = q_ref.shape[0] // chunk
    rows = [slice(c * chunk, (c + 1) * chunk) for c in range(n_chunks)]
    pending = [scores(rows[c]) for c in range(min(HGRN_LOOKAHEAD, n_chunks))]
    for c in range(n_chunks):
        if c + HGRN_LOOKAHEAD < n_chunks:
            pending.append(scores(rows[c + HGRN_LOOKAHEAD]))
        recur(rows[c], pending.pop(0))


def _hgrn(zh, w_lb, g_o, layer, batch, seq, chunk=HGRN_CHUNK,
          chunks_per_step=HGRN_CHUNKS_PER_STEP):
    t = zh.shape[0]
    rows = chunk * chunks_per_step
    ns = seq // rows
    spec = lambda grp: pl.BlockSpec((rows, D_HGRN), lambda b, c: (b * ns + c, grp))
    return pl.pallas_call(
        functools.partial(_hgrn_kernel, chunk=chunk, layer=layer),
        grid=(batch, ns),
        in_specs=[spec(0), spec(1), spec(2), spec(3),
                  pl.BlockSpec(w_lb.shape, lambda b, c: (0, 0)),
                  pl.BlockSpec((1, HEAD_DIM), lambda b, c: (0, 0))],
        out_specs=pl.BlockSpec((rows, D_HGRN), lambda b, c: (b * ns + c, 0)),
        out_shape=jax.ShapeDtypeStruct((t, D_HGRN), BF16),
        scratch_shapes=[pltpu.VMEM((HEADS, HEAD_DIM, HEAD_DIM), F32),
                        pltpu.VMEM((HEADS, chunk, HEAD_DIM), F32)],
        compiler_params=_params("arbitrary", "arbitrary"),
        name="hgrn2",
    )(zh, zh, zh, zh, w_lb, g_o.reshape(1, HEAD_DIM))


def _residual_norm_store(branch, rows, h_ref, gpost_ref, gnext_ref, h_out_ref, hn_out_ref):
    h = h_ref[rows, :] + _rms(branch, gpost_ref[...])
    h_out_ref[rows, :] = h
    if hn_out_ref is not None:
        hn_out_ref[rows, :] = _rms(h, gnext_ref[...]).astype(hn_out_ref.dtype)


def _subtile_rows(tm):
    sub = tm // ROW_SUBTILES
    return [slice(s * sub, (s + 1) * sub) for s in range(ROW_SUBTILES)]


def _outproj_kernel(ya_ref, yb_ref, wa_ref, wb_ref, h_ref, gpost_ref, gnext_ref,
                    h_out_ref, hn_out_ref):
    mixes = []
    for rows in _subtile_rows(h_ref.shape[0]):
        mix = jnp.dot(ya_ref[rows, :], wa_ref[...], preferred_element_type=F32)
        mixes.append(mix + jnp.dot(yb_ref[rows, :], wb_ref[...], preferred_element_type=F32))
    for rows, mix in zip(_subtile_rows(h_ref.shape[0]), mixes):
        _residual_norm_store(mix, rows, h_ref, gpost_ref, gnext_ref, h_out_ref, hn_out_ref)


def _outproj(ya, yb, w_out, layer, h, g_post, g_next):
    t, d = h.shape
    _, half_nb, _, tm, ka = ya.shape
    nb = 2 * half_nb
    kb = yb.shape[1]
    row = lambda width: pl.BlockSpec((tm, width), lambda i: (i, 0))
    gain = _resident((1, d), lambda i: (0, 0))

    def ya_index(i):
        b, tile = i // nb, i % nb
        second = tile >= half_nb
        return (b, jnp.where(second, nb - 1 - tile, tile), second.astype(jnp.int32), 0, 0)

    return pl.pallas_call(
        _outproj_kernel,
        grid=(t // tm,),
        in_specs=[pl.BlockSpec((None, None, None, tm, ka), ya_index), row(kb),
                  _resident((None, ka, d), lambda i: (layer, 0, 0)),
                  _resident((None, kb, d), lambda i: (layer, ka // kb, 0)),
                  row(d), gain, gain],
        out_specs=[row(d), row(d)],
        out_shape=[jax.ShapeDtypeStruct((t, d), F32), jax.ShapeDtypeStruct((t, d), BF16)],
        compiler_params=_params("arbitrary"),
        name="outproj",
    )(ya, yb, w_out, w_out, h, g_post.reshape(1, d), g_next.reshape(1, d))


def _upproj_kernel(x_ref, wg_ref, wu_ref, cwg_ref, cwu_ref, cbg_ref, cbu_ref, o_ref,
                   wgb_ref, wub_ref, ug_ref, uu_ref, *, tm, tiles_per_seq):
    i = pl.program_id(1)

    @pl.when(i == 0)
    def _():
        wgb_ref[...] = wg_ref[...].astype(BF16)
        wub_ref[...] = wu_ref[...].astype(BF16)

    @pl.when(i % tiles_per_seq == 0)
    def _():
        ug_ref[0:HALO, :] = jnp.zeros((HALO, ug_ref.shape[1]), F32)
        uu_ref[0:HALO, :] = jnp.zeros((HALO, uu_ref.shape[1]), F32)

    x = x_ref[...]
    ug_ref[HALO:tm + HALO, :] = jnp.dot(x, wgb_ref[...], preferred_element_type=F32)
    uu_ref[HALO:tm + HALO, :] = jnp.dot(x, wub_ref[...], preferred_element_type=F32)

    def conv(u_ref, cw_ref, cb_ref):
        acc = cb_ref[...]
        for j in reversed(range(CONV_WIDTH)):
            back = CONV_WIDTH - 1 - j
            acc = acc + cw_ref[j:j + 1, :] * u_ref[HALO - back:tm + HALO - back, :]
        return acc

    gate = conv(ug_ref, cwg_ref, cbg_ref)
    up = conv(uu_ref, cwu_ref, cbu_ref)
    o_ref[...] = (gate * (1.0 / (1.0 + jnp.exp(-gate))) * up).astype(o_ref.dtype)

    ug_ref[0:HALO, :] = ug_ref[tm:tm + HALO, :]
    uu_ref[0:HALO, :] = uu_ref[tm:tm + HALO, :]


def _upproj(xn, w_up, conv_w, conv_b, layer, seq, tm=1024, tn=512):
    t, k = xn.shape
    nj = D_FF // tn
    lay = lambda r, off: pl.BlockSpec((None, r, tn), lambda j, i: (layer, 0, j + off))
    return pl.pallas_call(
        functools.partial(_upproj_kernel, tm=tm, tiles_per_seq=seq // tm),
        grid=(nj, t // tm),
        in_specs=[pl.BlockSpec((tm, k), lambda j, i: (i, 0)),
                  lay(k, 0), lay(k, nj),
                  lay(CONV_WIDTH, 0), lay(CONV_WIDTH, nj),
                  lay(1, 0), lay(1, nj)],
        out_specs=pl.BlockSpec((tm, tn), lambda j, i: (i, j)),
        out_shape=jax.ShapeDtypeStruct((t, D_FF), BF16),
        scratch_shapes=[pltpu.VMEM((k, tn), BF16), pltpu.VMEM((k, tn), BF16),
                        pltpu.VMEM((tm + HALO, tn), F32), pltpu.VMEM((tm + HALO, tn), F32)],
        compiler_params=_params("arbitrary", "arbitrary"),
        name="upproj_conv",
    )(xn, w_up, w_up, conv_w, conv_w, conv_b, conv_b)


def _downproj_kernel(a_ref, w_ref, h_ref, gpost_ref, gnext_ref, h_out_ref, hn_out_ref):
    ff = jnp.dot(a_ref[...], w_ref[...], preferred_element_type=F32)
    _residual_norm_store(ff, slice(None), h_ref, gpost_ref, gnext_ref, h_out_ref, hn_out_ref)


def _downproj(a, w_down, layer, h, g_post, g_next, tm=256):
    t, d = h.shape
    k = a.shape[1]
    row = lambda width: pl.BlockSpec((tm, width), lambda i: (i, 0))
    gain = _resident((1, d), lambda i: (0, 0))
    return pl.pallas_call(
        _downproj_kernel,
        grid=(t // tm,),
        in_specs=[row(k), _resident((None, k, d), lambda i: (layer, 0, 0)), row(d), gain, gain],
        out_specs=[row(d), row(d)],
        out_shape=[jax.ShapeDtypeStruct((t, d), F32), jax.ShapeDtypeStruct((t, d), BF16)],
        compiler_params=_params("arbitrary"),
        name="downproj",
    )(a, w_down, h, g_post.reshape(1, d), g_next.reshape(1, d))


def _ple_kernel(p_ref, hn_ref, wp_ref, wg_ref, h_ref, gpost_ref, gnext_ref,
                h_out_ref, hn_out_ref=None):
    pairs = []
    for rows in _subtile_rows(h_ref.shape[0]):
        e = jnp.dot(p_ref[rows, :].astype(BF16), wp_ref[...], preferred_element_type=F32)
        z = jnp.dot(hn_ref[rows, :], wg_ref[...], preferred_element_type=F32)
        pairs.append((e, z))
    for rows, (e, z) in zip(_subtile_rows(h_ref.shape[0]), pairs):
        gated = e * (1.0 / (1.0 + jnp.exp(-z)))
        _residual_norm_store(gated, rows, h_ref, gpost_ref, gnext_ref, h_out_ref, hn_out_ref)


def _ple(p, hn, w_proj, w_gate, layer, h, g_post, g_next, tm=512):
    t, d = h.shape
    kp = p.shape[1]
    row = lambda width: pl.BlockSpec((tm, width), lambda i: (i, 0))
    gain = _resident((1, d), lambda i: (0, 0))
    last = g_next is None
    out_shape = [jax.ShapeDtypeStruct((t, d), F32)]
    if not last:
        out_shape.append(jax.ShapeDtypeStruct((t, d), BF16))
    outs = pl.pallas_call(
        _ple_kernel,
        grid=(t // tm,),
        in_specs=[row(kp), row(d),
                  _resident((None, kp, d), lambda i: (layer, 0, 0)),
                  _resident((None, d, d), lambda i: (layer, 0, 0)),
                  row(d), gain, gain],
        out_specs=[row(d)] * len(out_shape),
        out_shape=out_shape,
        compiler_params=_params("arbitrary"),
        name="ple",
    )(p, hn, w_proj, w_gate, h, g_post.reshape(1, d),
      (g_post if last else g_next).reshape(1, d))
    return (outs[0], None) if last else tuple(outs)


def kernel(x, p, g_mix_pre, w_in, b_fox_f, w_hgrn_lb, g_hgrn_out, w_out, g_mix_post,
           g_ffn_pre, w_up, conv_w, conv_b, w_down, g_ffn_post,
           g_ple_in, w_ple_gate, w_ple_proj, g_ple_post):
    batch, seq, d = x.shape
    depth = w_in.shape[0]
    t = batch * seq
    off_f = 3 * D_FOX
    off_hg = off_f + HEADS

    qscale = jnp.concatenate([jnp.full((1, D_FOX), LOG2E * HEAD_DIM ** -0.5, F32),
                              jnp.ones((1, 2 * D_FOX), F32)], axis=1)
    ones_hg = jnp.ones((1, 4 * D_HGRN), F32)
    w_in_t = jnp.swapaxes(w_in, 1, 2)
    w_f = jnp.pad(w_in[:, :, off_f:off_hg], ((0, 0), (0, 0), (0, LANES - HEADS))).astype(BF16)
    b_f = jnp.pad(b_fox_f, ((0, 0), (0, LANES - HEADS)))
    conv_b3 = conv_b.reshape(depth, 1, -1)
    w_out16, w_down16 = w_out.astype(BF16), w_down.astype(BF16)
    w_gate16, w_pproj16 = w_ple_gate.astype(BF16), w_ple_proj.astype(BF16)

    h = x.reshape(t, d)
    hn = None
    for i in range(depth):
        if hn is None:
            c3, hn = _fgate(h, w_f[i], b_f[i:i + 1], batch, seq, g_pre=g_mix_pre[i])
        else:
            c3 = _fgate(hn, w_f[i], b_f[i:i + 1], batch, seq)
        qkv = _proj(hn, w_in_t, i, 0, 3 * D_FOX, qscale, BF16, name="inproj_qkv")
        zh = _proj(hn, w_in_t, i, off_hg, 4 * D_HGRN, ones_hg, F32, name="inproj_hgrn")
        y_fox = _attention(qkv.reshape(batch, seq, 3 * D_FOX), c3, batch, seq)
        y_hg = _hgrn(zh, w_hgrn_lb, g_hgrn_out[i], i, batch, seq)

        h, hn = _outproj(y_fox, y_hg, w_out16, i, h, g_mix_post[i], g_ffn_pre[i])
        a = _upproj(hn, w_up, conv_w, conv_b3, i, seq)
        h, hn = _downproj(a, w_down16, i, h, g_ffn_post[i], g_ple_in[i])
        g_next = g_mix_pre[i + 1] if i + 1 < depth else None
        h, hn = _ple(p[i].reshape(t, -1), hn, w_pproj16, w_gate16, i, h, g_ple_post[i], g_next)
    return h.reshape(batch, seq, d)
```

```python
import functools
import math

import jax
import jax.numpy as jnp
from jax import lax
from jax.experimental import pallas as pl
from jax.experimental.pallas import tpu as pltpu

F32 = jnp.float32
BF16 = jnp.bfloat16

D_MODEL = 2048
D_FOX = 1024
D_HGRN = 1024
HEADS = 8
HEAD_DIM = 128
D_FF = 5632
CONV_WIDTH = 3
EPS = 1e-6
LOG2E = math.log2(math.e)
LANES = 128
SUBLANES = 8
HALO = SUBLANES
DEN_ROWS = SUBLANES

HGRN_CHUNK = 64
HGRN_CHUNKS_PER_STEP = 8
HGRN_LOOKAHEAD = 1
ATT_BLOCK = 512
ATT_GROUP = 4
ATT_LOOKAHEAD = 2
ROW_SUBTILES = 4

V7X_VMEM_BYTES = 64 * 1024 * 1024
VMEM_LIMIT = V7X_VMEM_BYTES - 8 * 1024 * 1024


def _params(*semantics):
    return pltpu.CompilerParams(dimension_semantics=semantics,
                                vmem_limit_bytes=VMEM_LIMIT)


def _rms(x, g):
    ms = jnp.mean(x * x, axis=-1, keepdims=True)
    return (x * lax.rsqrt(ms + EPS)) * g


def _log_sigmoid(x):
    return jnp.minimum(x, 0.0) - jnp.log(1.0 + jnp.exp(-jnp.abs(x)))


def _resident(shape, index_map):
    return pl.BlockSpec(shape, index_map, pipeline_mode=pl.Buffered(1))


def _proj_kernel(x_ref, w_ref, wtail_ref, s_ref, o_ref, wb_ref, *, shift):
    @pl.when(pl.program_id(1) == 0)
    def _():
        if shift:
            w = jnp.concatenate([w_ref[...], wtail_ref[...]], axis=0)
            wb_ref[...] = w[shift:shift + wb_ref.shape[0], :].astype(BF16)
        else:
            wb_ref[...] = w_ref[...].astype(BF16)

    acc = lax.dot_general(x_ref[...], wb_ref[...], (((1,), (1,)), ((), ())),
                          preferred_element_type=F32)
    o_ref[...] = (acc * s_ref[...]).astype(o_ref.dtype)


def _proj(x, w_t, layer, row0, n, colscale, out_dtype, tm=1024, tn=1024, name="proj"):
    t, k = x.shape
    shift = row0 % tn
    base = row0 - shift
    assert shift in (0, SUBLANES) and n % tn == 0 and tn % SUBLANES == 0
    tail = lambda j, i: (layer, (base + (j + 1) * tn) // SUBLANES if shift else 0, 0)
    return pl.pallas_call(
        functools.partial(_proj_kernel, shift=shift),
        grid=(n // tn, t // tm),
        in_specs=[pl.BlockSpec((tm, k), lambda j, i: (i, 0)),
                  pl.BlockSpec((None, tn, k), lambda j, i: (layer, base // tn + j, 0)),
                  pl.BlockSpec((None, SUBLANES, k), tail),
                  pl.BlockSpec((1, tn), lambda j, i: (0, j))],
        out_specs=pl.BlockSpec((tm, tn), lambda j, i: (i, j)),
        out_shape=jax.ShapeDtypeStruct((t, n), out_dtype),
        scratch_shapes=[pltpu.VMEM((tn, k), BF16)],
        compiler_params=_params("arbitrary", "arbitrary"),
        name=name,
    )(x, w_t, w_t, colscale)


def _fgate_kernel(*refs, tm, normalize):
    if normalize:
        x_ref, g_ref, w_ref, b_ref, o_ref, xn_ref, carry_ref = refs
        xn = _rms(x_ref[...], g_ref[...]).astype(BF16)
        xn_ref[...] = xn
    else:
        x_ref, w_ref, b_ref, o_ref, carry_ref = refs
        xn = x_ref[...]

    @pl.when(pl.program_id(1) == 0)
    def _():
        carry_ref[...] = jnp.zeros_like(carry_ref)

    f = jnp.dot(xn, w_ref[...], preferred_element_type=F32) + b_ref[...]
    c = _log_sigmoid(f)
    row = lax.broadcasted_iota(jnp.int32, c.shape, 0)
    d = 1
    while d < tm:
        c = c + jnp.where(row >= d, pltpu.roll(c, d, 0), 0.0)
        d *= 2
    c = c + carry_ref[0:1, :]
    carry_ref[...] = jnp.broadcast_to(c[tm - 1:tm, :], carry_ref.shape)

    x = c * (-LOG2E)
    hi = x.astype(BF16).astype(F32)
    rest = x - hi
    mid = rest.astype(BF16).astype(F32)
    lo = rest - mid
    lane = lax.broadcasted_iota(jnp.int32, c.shape, 1)
    parts = jnp.where(lane < HEADS, hi,
                      jnp.where(lane < 2 * HEADS, pltpu.roll(mid, HEADS, 1),
                                jnp.where(lane < 3 * HEADS, pltpu.roll(lo, 2 * HEADS, 1), 0.0)))
    o_ref[0] = parts.astype(BF16)


def _fgate(x, w_f, b_f, batch, seq, g_pre=None, tm=512):
    t, k = x.shape
    nt = seq // tm
    normalize = g_pre is not None
    rows = pl.BlockSpec((tm, k), lambda b, s: (b * nt + s, 0))
    in_specs = [rows] + ([pl.BlockSpec((1, k), lambda b, s: (0, 0))] if normalize else []) + [
        pl.BlockSpec((k, LANES), lambda b, s: (0, 0)),
        pl.BlockSpec((1, LANES), lambda b, s: (0, 0))]
    out_specs = [pl.BlockSpec((1, tm, LANES), lambda b, s: (b, s, 0))]
    out_shape = [jax.ShapeDtypeStruct((batch, seq, LANES), BF16)]
    if normalize:
        out_specs.append(rows)
        out_shape.append(jax.ShapeDtypeStruct((t, k), BF16))
    args = (x, g_pre.reshape(1, k), w_f, b_f) if normalize else (x, w_f, b_f)
    outs = pl.pallas_call(
        functools.partial(_fgate_kernel, tm=tm, normalize=normalize),
        grid=(batch, nt),
        in_specs=in_specs,
        out_specs=out_specs,
        out_shape=out_shape,
        scratch_shapes=[pltpu.VMEM((SUBLANES, LANES), F32)],
        compiler_params=_params("arbitrary", "arbitrary"),
        name="fgate",
    )(*args)
    return tuple(outs) if normalize else outs[0]


def _attn_kernel(qa_ref, qb_ref, k_ref, v_ref, c_ref, o_ref, vt_ref, q_scr, m_ref, acc_ref,
                 *, blk, nb, group):
    hg, ip = pl.program_id(1), pl.program_id(2)
    head_cols = [slice(hh * HEAD_DIM, (hh + 1) * HEAD_DIM) for hh in range(group)]

    @pl.when(ip == 0)
    def _():
        for hh in range(group):
            for r in range(nb):
                vblk = v_ref[0, r * blk:(r + 1) * blk, head_cols[hh]]
                vt_ref[hh * nb + r, 0:HEAD_DIM, :] = vblk.astype(F32).T.astype(BF16)
                vt_ref[hh * nb + r, HEAD_DIM:HEAD_DIM + DEN_ROWS, :] = jnp.ones((DEN_ROWS, blk),
                                                                                BF16)

    m_ref[...] = jnp.full(m_ref.shape, -jnp.inf, F32)
    acc_ref[...] = jnp.zeros(acc_ref.shape, F32)

    lane = lax.broadcasted_iota(jnp.int32, (blk, HEAD_DIM), 1)
    for hh in range(group):
        head = hg * group + hh
        gate_lanes = (lane == head) | (lane == head + HEADS) | (lane == head + 2 * HEADS)
        ones = jnp.where(gate_lanes, 1.0, 0.0).astype(BF16)
        for tile, q_ref in enumerate((qa_ref, qb_ref)):
            q_scr[group * tile + hh] = jnp.concatenate([q_ref[0, :, head_cols[hh]], ones], axis=1)

    def item(t):
        if t == 0:
            return 0, ip, True
        if t == nb:
            return 1, nb - 1 - ip, True
        if t >= nb // 2:
            return 1, t - 1 - ip, False
        in_b = t > ip
        return in_b.astype(jnp.int32), jnp.where(in_b, t - 1 - ip, t - 1), False

    half = blk // 2
    halves = (slice(0, half), slice(half, blk))

    def nt_dot(a, b):
        return lax.dot_general(a, b, (((1,), (1,)), ((), ())), preferred_element_type=F32)

    def scores(t, heads):
        tile, kb, diag = item(t)
        ks = pl.multiple_of(kb * blk, blk)
        gate = c_ref[0, pl.ds(ks, blk), :]
        out = {}
        for hh in heads:
            k_aug = jnp.concatenate([k_ref[0, pl.ds(ks, blk), head_cols[hh]], gate], axis=1)
            q_aug = q_scr[group * tile + hh]
            if not diag:
                s = nt_dot(k_aug, q_aug)
                out[hh] = ((s,), (jnp.max(s, axis=0, keepdims=True),))
                continue
            k_lo, k_hi = k_aug[0:half], k_aug[half:blk]
            q_lo, q_hi = q_aug[0:half], q_aug[half:blk]
            key = lax.broadcasted_iota(jnp.int32, (half, half), 0)
            qry = lax.broadcasted_iota(jnp.int32, (half, half), 1)
            first = jnp.where(key <= qry, nt_dot(k_lo, q_lo), -jnp.inf)
            above = nt_dot(k_lo, q_hi)
            last = jnp.where(key <= qry, nt_dot(k_hi, q_hi), -jnp.inf)
            right = jnp.maximum(jnp.max(above, axis=0, keepdims=True),
                                jnp.max(last, axis=0, keepdims=True))
            out[hh] = ((first, above, last), (jnp.max(first, axis=0, keepdims=True), right))
        return out

    def softmax_pv(t, scored):
        tile, kb, _ = item(t)
        for hh in scored:
            idx = group * tile + hh
            pieces, bmax = scored[hh]
            if len(pieces) == 1:
                m_old = m_ref[idx]
                m_new = jnp.maximum(m_old, bmax[0])
                alpha = jnp.exp2(m_old - m_new)
                p = jnp.exp2(pieces[0] - m_new)
                pv = jnp.dot(vt_ref[hh * nb + kb], p.astype(BF16), preferred_element_type=F32)
                acc_ref[idx] = alpha * acc_ref[idx] + pv
                m_ref[idx] = m_new
                continue
            first, above, last = pieces
            for side, cols in enumerate(halves):
                m_old = m_ref[idx, :, cols]
                m_new = jnp.maximum(m_old, bmax[side])
                alpha = jnp.exp2(m_old - m_new)
                if side == 0:
                    p = jnp.exp2(first - m_new).astype(BF16)
                    pv = jnp.dot(vt_ref[hh * nb + kb, :, 0:half], p, preferred_element_type=F32)
                else:
                    p = jnp.concatenate([jnp.exp2(above - m_new), jnp.exp2(last - m_new)],
                                        axis=0).astype(BF16)
                    pv = jnp.dot(vt_ref[hh * nb + kb], p, preferred_element_type=F32)
                acc_ref[idx, :, cols] = alpha * acc_ref[idx, :, cols] + pv
                m_ref[idx, :, cols] = m_new

    pending = [scores(t, range(group)) for t in range(ATT_LOOKAHEAD)]
    for t in range(nb + 1):
        if t + ATT_LOOKAHEAD < nb + 1:
            pending.append(scores(t + ATT_LOOKAHEAD, range(group)))
        softmax_pv(t, pending.pop(0))

    for tile in range(2):
        for hh in range(group):
            acc = acc_ref[group * tile + hh]
            o = (acc[0:HEAD_DIM] / acc[HEAD_DIM:HEAD_DIM + 1]).T
            o_ref[0, 0, tile, :, head_cols[hh]] = o.astype(o_ref.dtype)


def _attention(qkv, gate, batch, seq, blk=ATT_BLOCK, group=ATT_GROUP):
    nb = seq // blk
    width = group * HEAD_DIM
    ngroup = HEADS // group
    return pl.pallas_call(
        functools.partial(_attn_kernel, blk=blk, nb=nb, group=group),
        grid=(batch, ngroup, nb // 2),
        in_specs=[
            pl.BlockSpec((1, blk, width), lambda b, h, i: (b, i, h)),
            pl.BlockSpec((1, blk, width), lambda b, h, i: (b, nb - 1 - i, h)),
            pl.BlockSpec((1, seq, width), lambda b, h, i: (b, 0, ngroup + h)),
            pl.BlockSpec((1, seq, width), lambda b, h, i: (b, 0, 2 * ngroup + h)),
            pl.BlockSpec((1, seq, LANES), lambda b, h, i: (b, 0, 0)),
        ],
        out_specs=pl.BlockSpec((1, 1, 2, blk, width), lambda b, h, i: (b, i, 0, 0, h)),
        out_shape=jax.ShapeDtypeStruct((batch, nb // 2, 2, blk, D_FOX), BF16),
        scratch_shapes=[pltpu.VMEM((group * nb, HEAD_DIM + DEN_ROWS, blk), BF16),
                        pltpu.VMEM((2 * group, blk, 2 * HEAD_DIM), BF16),
                        pltpu.VMEM((2 * group, 1, blk), F32),
                        pltpu.VMEM((2 * group, HEAD_DIM + DEN_ROWS, blk), F32)],
        compiler_params=_params("arbitrary", "arbitrary", "arbitrary"),
        name="fox_attention",
    )(qkv, qkv, qkv, qkv, gate)


def _block_middle_rows(cum, cum_ref, chunk):
    row = lax.broadcasted_iota(jnp.int32, cum.shape, 0)
    spread = lambda r: jnp.broadcast_to(cum_ref[r:r + 1, :], (SUBLANES, cum.shape[-1]))
    out = []
    h = chunk // 2
    while h >= 4:
        reps = max(2 * h // SUBLANES, 1)
        r = jnp.concatenate([spread(b + h - 1) for b in range(0, chunk, 2 * h)
                             for _ in range(reps)], axis=0)
        out.append((h, r))
        h //= 2
    low = jnp.concatenate([spread(b + 1) for b in range(0, chunk, SUBLANES)], axis=0)
    high = jnp.concatenate([spread(b + 5) for b in range(0, chunk, SUBLANES)], axis=0)
    out.append((2, jnp.where(row % SUBLANES < 4, low, high)))
    tiles = cum.reshape(chunk // SUBLANES, SUBLANES, cum.shape[-1])
    above = pltpu.roll(tiles, 1, 1).reshape(cum.shape)
    out.append((1, jnp.where(row % 2 == 1, above, cum)))
    return out


def _hgrn_kernel(q_ref, f_ref, i_ref, g_ref, wlb_ref, go_ref, o_ref, state_ref, cum_ref,
                 *, chunk, layer):
    @pl.when(pl.program_id(1) == 0)
    def _():
        state_ref[...] = jnp.zeros_like(state_ref)

    if layer > 0:
        w = wlb_ref[...]
        ew = jnp.exp(w - jnp.max(w, axis=0, keepdims=True))
        sm = ew / jnp.sum(ew, axis=0, keepdims=True)
        lb_all = sm[1:2]
        for dpt in range(2, layer + 1):
            lb_all = lb_all + sm[dpt:dpt + 1]
        log_lb_all = jnp.log(lb_all)
        log_1mlb_all = jnp.log1p(-lb_all)

    row = lax.broadcasted_iota(jnp.int32, (chunk, HEAD_DIM), 0)
    tile_row = lax.broadcasted_iota(jnp.int32, (chunk // SUBLANES, SUBLANES, HEAD_DIM), 1)
    rr = lax.broadcasted_iota(jnp.int32, (chunk, chunk), 0)
    cc = lax.broadcasted_iota(jnp.int32, (chunk, chunk), 1)
    pair_mask = {}
    side = {}
    h = chunk // 2
    while h >= 1:
        pair_mask[h] = (((rr // (2 * h)) == (cc // (2 * h)))
                        & (rr % (2 * h) >= h) & (cc % (2 * h) < h))
        side[h] = jnp.where(row % (2 * h) >= h, 1.0, -1.0)
        h //= 2

    def scores(rows):
        stage = []
        for hd in range(HEADS):
            sl = slice(hd * HEAD_DIM, (hd + 1) * HEAD_DIM)
            q = q_ref[rows, sl]
            f = f_ref[rows, sl]
            v = i_ref[rows, sl]
            if layer == 0:
                log_f = _log_sigmoid(f)
                k = 1.0 / (1.0 + jnp.exp(f))
            else:
                a = log_lb_all[:, sl]
                b = log_1mlb_all[:, sl] + _log_sigmoid(f)
                log_f = jnp.maximum(a, b) + jnp.log(1.0 + jnp.exp(-jnp.abs(a - b)))
                k = (1.0 - lb_all[:, sl]) * (1.0 / (1.0 + jnp.exp(f)))

            cum = log_f * LOG2E
            tiles = cum.reshape(chunk // SUBLANES, SUBLANES, HEAD_DIM)
            d = 1
            while d < SUBLANES:
                tiles = tiles + jnp.where(tile_row >= d, pltpu.roll(tiles, d, 1), 0.0)
                d *= 2
            offset, offsets = None, [jnp.zeros((1, 1, HEAD_DIM), F32)]
            for tile in range(chunk // SUBLANES - 1):
                total = tiles[tile:tile + 1, SUBLANES - 1:SUBLANES, :]
                offset = total if offset is None else offset + total
                offsets.append(offset)
            cum = (tiles + jnp.concatenate(offsets, axis=0)).reshape(chunk, HEAD_DIM)
            last = cum[chunk - 1:chunk, :]

            sc = jnp.zeros((chunk, chunk), F32)
            q16, k16 = q.astype(BF16), k.astype(BF16)
            cum_ref[hd] = cum
            for h, r in _block_middle_rows(cum, cum_ref.at[hd], chunk):
                decay = jnp.exp2((cum - r) * side[h]).astype(BF16)
                s_l = lax.dot_general(q16 * decay, k16 * decay,
                                      (((1,), (1,)), ((), ())), preferred_element_type=F32)
                sc = jnp.where(pair_mask[h], s_l, sc)
            stage.append((sc.astype(BF16), v.astype(BF16),
                          jnp.sum(q * k, axis=-1, keepdims=True) * v,
                          (q * jnp.exp2(cum)).astype(BF16),
                          (k * jnp.exp2(last - cum)).astype(BF16), jnp.exp2(last)))
        return stage

    def recur(rows, stage):
        for hd in range(HEADS):
            sl = slice(hd * HEAD_DIM, (hd + 1) * HEAD_DIM)
            sc16, v16, o_diag, qd, kd, decay_last = stage[hd]
            st = state_ref[hd]
            o = o_diag + jnp.dot(sc16, v16, preferred_element_type=F32)
            o = o + lax.dot_general(qd, st.astype(BF16), (((1,), (1,)), ((), ())),
                                    preferred_element_type=F32)
            upd = lax.dot_general(v16, kd, (((0,), (0,)), ((), ())),
                                  preferred_element_type=F32)
            state_ref[hd] = st * decay_last + upd

            g = g_ref[rows, sl]
            y = _rms(o, go_ref[...]) * (g * (1.0 / (1.0 + jnp.exp(-g))))
            o_ref[rows, sl] = y.astype(o_ref.dtype)

    n_chunks = q_ref.shape[0] // chunk
    rows = [slice(c * chunk, (c + 1) * chunk) for c in range(n_chunks)]
    pending = [scores(rows[c]) for c in range(min(HGRN_LOOKAHEAD, n_chunks))]
    for c in range(n_chunks):
        if c + HGRN_LOOKAHEAD < n_chunks:
            pending.append(scores(rows[c + HGRN_LOOKAHEAD]))
        recur(rows[c], pending.pop(0))


def _hgrn(zh, w_lb, g_o, layer, batch, seq, chunk=HGRN_CHUNK,
          chunks_per_step=HGRN_CHUNKS_PER_STEP):
    t = zh.shape[0]
    rows = chunk * chunks_per_step
    ns = seq // rows
    spec = lambda grp: pl.BlockSpec((rows, D_HGRN), lambda b, c: (b * ns + c, grp))
    return pl.pallas_call(
        functools.partial(_hgrn_kernel, chunk=chunk, layer=layer),
        grid=(batch, ns),
        in_specs=[spec(0), spec(1), spec(2), spec(3),
                  pl.BlockSpec(w_lb.shape, lambda b, c: (0, 0)),
                  pl.BlockSpec((1, HEAD_DIM), lambda b, c: (0, 0))],
        out_specs=pl.BlockSpec((rows, D_HGRN), lambda b, c: (b * ns + c, 0)),
        out_shape=jax.ShapeDtypeStruct((t, D_HGRN), BF16),
        scratch_shapes=[pltpu.VMEM((HEADS, HEAD_DIM, HEAD_DIM), F32),
                        pltpu.VMEM((HEADS, chunk, HEAD_DIM), F32)],
        compiler_params=_params("arbitrary", "arbitrary"),
        name="hgrn2",
    )(zh, zh, zh, zh, w_lb, g_o.reshape(1, HEAD_DIM))


def _residual_norm_store(branch, rows, h_ref, gpost_ref, gnext_ref, h_out_ref, hn_out_ref):
    h = h_ref[rows, :] + _rms(branch, gpost_ref[...])
    h_out_ref[rows, :] = h
    if hn_out_ref is not None:
        hn_out_ref[rows, :] = _rms(h, gnext_ref[...]).astype(hn_out_ref.dtype)


def _subtile_rows(tm):
    sub = tm // ROW_SUBTILES
    return [slice(s * sub, (s + 1) * sub) for s in range(ROW_SUBTILES)]


def _outproj_kernel(ya_ref, yb_ref, wa_ref, wb_ref, h_ref, gpost_ref, gnext_ref,
                    h_out_ref, hn_out_ref):
    mixes = []
    for rows in _subtile_rows(h_ref.shape[0]):
        mix = jnp.dot(ya_ref[rows, :], wa_ref[...], preferred_element_type=F32)
        mixes.append(mix + jnp.dot(yb_ref[rows, :], wb_ref[...], preferred_element_type=F32))
    for rows, mix in zip(_subtile_rows(h_ref.shape[0]), mixes):
        _residual_norm_store(mix, rows, h_ref, gpost_ref, gnext_ref, h_out_ref, hn_out_ref)


def _outproj(ya, yb, w_out, layer, h, g_post, g_next):
    t, d = h.shape
    _, half_nb, _, tm, ka = ya.shape
    nb = 2 * half_nb
    kb = yb.shape[1]
    row = lambda width: pl.BlockSpec((tm, width), lambda i: (i, 0))
    gain = _resident((1, d), lambda i: (0, 0))

    def ya_index(i):
        b, tile = i // nb, i % nb
        second = tile >= half_nb
        return (b, jnp.where(second, nb - 1 - tile, tile), second.astype(jnp.int32), 0, 0)

    return pl.pallas_call(
        _outproj_kernel,
        grid=(t // tm,),
        in_specs=[pl.BlockSpec((None, None, None, tm, ka), ya_index), row(kb),
                  _resident((None, ka, d), lambda i: (layer, 0, 0)),
                  _resident((None, kb, d), lambda i: (layer, ka // kb, 0)),
                  row(d), gain, gain],
        out_specs=[row(d), row(d)],
        out_shape=[jax.ShapeDtypeStruct((t, d), F32), jax.ShapeDtypeStruct((t, d), BF16)],
        compiler_params=_params("arbitrary"),
        name="outproj",
    )(ya, yb, w_out, w_out, h, g_post.reshape(1, d), g_next.reshape(1, d))


def _upproj_kernel(x_ref, wg_ref, wu_ref, cwg_ref, cwu_ref, cbg_ref, cbu_ref, o_ref,
                   wgb_ref, wub_ref, ug_ref, uu_ref, *, tm, tiles_per_seq):
    i = pl.program_id(1)

    @pl.when(i == 0)
    def _():
        wgb_ref[...] = wg_ref[...].astype(BF16)
        wub_ref[...] = wu_ref[...].astype(BF16)

    @pl.when(i % tiles_per_seq == 0)
    def _():
        ug_ref[0:HALO, :] = jnp.zeros((HALO, ug_ref.shape[1]), F32)
        uu_ref[0:HALO, :] = jnp.zeros((HALO, uu_ref.shape[1]), F32)

    x = x_ref[...]
    ug_ref[HALO:tm + HALO, :] = jnp.dot(x, wgb_ref[...], preferred_element_type=F32)
    uu_ref[HALO:tm + HALO, :] = jnp.dot(x, wub_ref[...], preferred_element_type=F32)

    def conv(u_ref, cw_ref, cb_ref):
        acc = cb_ref[...]
        for j in reversed(range(CONV_WIDTH)):
            back = CONV_WIDTH - 1 - j
            acc = acc + cw_ref[j:j + 1, :] * u_ref[HALO - back:tm + HALO - back, :]
        return acc

    gate = conv(ug_ref, cwg_ref, cbg_ref)
    up = conv(uu_ref, cwu_ref, cbu_ref)
    o_ref[...] = (gate * (1.0 / (1.0 + jnp.exp(-gate))) * up).astype(o_ref.dtype)

    ug_ref[0:HALO, :] = ug_ref[tm:tm + HALO, :]
    uu_ref[0:HALO, :] = uu_ref[tm:tm + HALO, :]


def _upproj(xn, w_up, conv_w, conv_b, layer, seq, tm=1024, tn=512):
    t, k = xn.shape
    nj = D_FF // tn
    lay = lambda r, off: pl.BlockSpec((None, r, tn), lambda j, i: (layer, 0, j + off))
    return pl.pallas_call(
        functools.partial(_upproj_kernel, tm=tm, tiles_per_seq=seq // tm),
        grid=(nj, t // tm),
        in_specs=[pl.BlockSpec((tm, k), lambda j, i: (i, 0)),
                  lay(k, 0), lay(k, nj),
                  lay(CONV_WIDTH, 0), lay(CONV_WIDTH, nj),
                  lay(1, 0), lay(1, nj)],
        out_specs=pl.BlockSpec((tm, tn), lambda j, i: (i, j)),
        out_shape=jax.ShapeDtypeStruct((t, D_FF), BF16),
        scratch_shapes=[pltpu.VMEM((k, tn), BF16), pltpu.VMEM((k, tn), BF16),
                        pltpu.VMEM((tm + HALO, tn), F32), pltpu.VMEM((tm + HALO, tn), F32)],
        compiler_params=_params("arbitrary", "arbitrary"),
        name="upproj_conv",
    )(xn, w_up, w_up, conv_w, conv_w, conv_b, conv_b)


def _downproj_kernel(a_ref, w_ref, h_ref, gpost_ref, gnext_ref, h_out_ref, hn_out_ref):
    ff = jnp.dot(a_ref[...], w_ref[...], preferred_element_type=F32)
    _residual_norm_store(ff, slice(None), h_ref, gpost_ref, gnext_ref, h_out_ref, hn_out_ref)


def _downproj(a, w_down, layer, h, g_post, g_next, tm=256):
    t, d = h.shape
    k = a.shape[1]
    row = lambda width: pl.BlockSpec((tm, width), lambda i: (i, 0))
    gain = _resident((1, d), lambda i: (0, 0))
    return pl.pallas_call(
        _downproj_kernel,
        grid=(t // tm,),
        in_specs=[row(k), _resident((None, k, d), lambda i: (layer, 0, 0)), row(d), gain, gain],
        out_specs=[row(d), row(d)],
        out_shape=[jax.ShapeDtypeStruct((t, d), F32), jax.ShapeDtypeStruct((t, d), BF16)],
        compiler_params=_params("arbitrary"),
        name="downproj",
    )(a, w_down, h, g_post.reshape(1, d), g_next.reshape(1, d))


def _ple_kernel(p_ref, hn_ref, wp_ref, wg_ref, h_ref, gpost_ref, gnext_ref,
                h_out_ref, hn_out_ref=None):
    pairs = []
    for rows in _subtile_rows(h_ref.shape[0]):
        e = jnp.dot(p_ref[rows, :].astype(BF16), wp_ref[...], preferred_element_type=F32)
        z = jnp.dot(hn_ref[rows, :], wg_ref[...], preferred_element_type=F32)
        pairs.append((e, z))
    for rows, (e, z) in zip(_subtile_rows(h_ref.shape[0]), pairs):
        gated = e * (1.0 / (1.0 + jnp.exp(-z)))
        _residual_norm_store(gated, rows, h_ref, gpost_ref, gnext_ref, h_out_ref, hn_out_ref)


def _ple(p, hn, w_proj, w_gate, layer, h, g_post, g_next, tm=512):
    t, d = h.shape
    kp = p.shape[1]
    row = lambda width: pl.BlockSpec((tm, width), lambda i: (i, 0))
    gain = _resident((1, d), lambda i: (0, 0))
    last = g_next is None
    out_shape = [jax.ShapeDtypeStruct((t, d), F32)]
    if not last:
        out_shape.append(jax.ShapeDtypeStruct((t, d), BF16))
    outs = pl.pallas_call(
        _ple_kernel,
        grid=(t // tm,),
        in_specs=[row(kp), row(d),
                  _resident((None, kp, d), lambda i: (layer, 0, 0)),
                  _resident((None, d, d), lambda i: (layer, 0, 0)),
                  row(d), gain, gain],
        out_specs=[row(d)] * len(out_shape),
        out_shape=out_shape,
        compiler_params=_params("arbitrary"),
        name="ple",
    )(p, hn, w_proj, w_gate, h, g_post.reshape(1, d),
      (g_post if last else g_next).reshape(1, d))
    return (outs[0], None) if last else tuple(outs)


def kernel(x, p, g_mix_pre, w_in, b_fox_f, w_hgrn_lb, g_hgrn_out, w_out, g_mix_post,
           g_ffn_pre, w_up, conv_w, conv_b, w_down, g_ffn_post,
           g_ple_in, w_ple_gate, w_ple_proj, g_ple_post):
    batch, seq, d = x.shape
    depth = w_in.shape[0]
    t = batch * seq
    off_f = 3 * D_FOX
    off_hg = off_f + HEADS

    qscale = jnp.concatenate([jnp.full((1, D_FOX), LOG2E * HEAD_DIM ** -0.5, F32),
                              jnp.ones((1, 2 * D_FOX), F32)], axis=1)
    ones_hg = jnp.ones((1, 4 * D_HGRN), F32)
    w_in_t = jnp.swapaxes(w_in, 1, 2)
    w_f = jnp.pad(w_in[:, :, off_f:off_hg], ((0, 0), (0, 0), (0, LANES - HEADS))).astype(BF16)
    b_f = jnp.pad(b_fox_f, ((0, 0), (0, LANES - HEADS)))
    conv_b3 = conv_b.reshape(depth, 1, -1)
    w_out16, w_down16 = w_out.astype(BF16), w_down.astype(BF16)
    w_gate16, w_pproj16 = w_ple_gate.astype(BF16), w_ple_proj.astype(BF16)

    h = x.reshape(t, d)
    hn = None
    for i in range(depth):
        if hn is None:
            c3, hn = _fgate(h, w_f[i], b_f[i:i + 1], batch, seq, g_pre=g_mix_pre[i])
        else:
            c3 = _fgate(hn, w_f[i], b_f[i:i + 1], batch, seq)
        qkv = _proj(hn, w_in_t, i, 0, 3 * D_FOX, qscale, BF16, name="inproj_qkv")
        zh = _proj(hn, w_in_t, i, off_hg, 4 * D_HGRN, ones_hg, F32, name="inproj_hgrn")
        y_fox = _attention(qkv.reshape(batch, seq, 3 * D_FOX), c3, batch, seq)
        y_hg = _hgrn(zh, w_hgrn_lb, g_hgrn_out[i], i, batch, seq)

        h, hn = _outproj(y_fox, y_hg, w_out16, i, h, g_mix_post[i], g_ffn_pre[i])
        a = _upproj(hn, w_up, conv_w, conv_b3, i, seq)
        h, hn = _downproj(a, w_down16, i, h, g_ffn_post[i], g_ple_in[i])
        g_next = g_mix_pre[i + 1] if i + 1 < depth else None
        h, hn = _ple(p[i].reshape(t, -1), hn, w_pproj16, w_gate16, i, h, g_ple_post[i], g_next)
    return h.reshape(batch, seq, d)
```

```python
import functools
import math

import jax
import jax.numpy as jnp
from jax import lax
from jax.experimental import pallas as pl
from jax.experimental.pallas import tpu as pltpu

F32 = jnp.float32
BF16 = jnp.bfloat16

D_MODEL = 2048
D_FOX = 1024
D_HGRN = 1024
HEADS = 8
HEAD_DIM = 128
D_FF = 5632
CONV_WIDTH = 3
EPS = 1e-6
LOG2E = math.log2(math.e)
LANES = 128
SUBLANES = 8
HALO = SUBLANES
DEN_ROWS = SUBLANES

HGRN_CHUNK = 64
HGRN_CHUNKS_PER_STEP = 8
HGRN_LOOKAHEAD = 1
ATT_BLOCK = 512
ATT_GROUP = 4
ATT_LOOKAHEAD = 2
ROW_SUBTILES = 4

V7X_VMEM_BYTES = 64 * 1024 * 1024
VMEM_LIMIT = V7X_VMEM_BYTES - 8 * 1024 * 1024


def _params(*semantics):
    return pltpu.CompilerParams(dimension_semantics=semantics,
                                vmem_limit_bytes=VMEM_LIMIT)


def _rms(x, g):
    ms = jnp.mean(x * x, axis=-1, keepdims=True)
    return (x * lax.rsqrt(ms + EPS)) * g


def _log_sigmoid(x):
    return jnp.minimum(x, 0.0) - jnp.log(1.0 + jnp.exp(-jnp.abs(x)))


def _resident(shape, index_map):
    return pl.BlockSpec(shape, index_map, pipeline_mode=pl.Buffered(1))


def _proj_kernel(x_ref, w_ref, wtail_ref, s_ref, o_ref, wb_ref, *, shift):
    @pl.when(pl.program_id(1) == 0)
    def _():
        if shift:
            w = jnp.concatenate([w_ref[...], wtail_ref[...]], axis=0)
            wb_ref[...] = w[shift:shift + wb_ref.shape[0], :].astype(BF16)
        else:
            wb_ref[...] = w_ref[...].astype(BF16)

    acc = lax.dot_general(x_ref[...], wb_ref[...], (((1,), (1,)), ((), ())),
                          preferred_element_type=F32)
    o_ref[...] = (acc * s_ref[...]).astype(o_ref.dtype)


def _proj(x, w_t, layer, row0, n, colscale, out_dtype, tm=1024, tn=1024, name="proj"):
    t, k = x.shape
    shift = row0 % tn
    base = row0 - shift
    assert shift in (0, SUBLANES) and n % tn == 0 and tn % SUBLANES == 0
    tail = lambda j, i: (layer, (base + (j + 1) * tn) // SUBLANES if shift else 0, 0)
    return pl.pallas_call(
        functools.partial(_proj_kernel, shift=shift),
        grid=(n // tn, t // tm),
        in_specs=[pl.BlockSpec((tm, k), lambda j, i: (i, 0)),
                  pl.BlockSpec((None, tn, k), lambda j, i: (layer, base // tn + j, 0)),
                  pl.BlockSpec((None, SUBLANES, k), tail),
                  pl.BlockSpec((1, tn), lambda j, i: (0, j))],
        out_specs=pl.BlockSpec((tm, tn), lambda j, i: (i, j)),
        out_shape=jax.ShapeDtypeStruct((t, n), out_dtype),
        scratch_shapes=[pltpu.VMEM((tn, k), BF16)],
        compiler_params=_params("arbitrary", "arbitrary"),
        name=name,
    )(x, w_t, w_t, colscale)


def _fgate_kernel(*refs, tm, normalize):
    if normalize:
        x_ref, g_ref, w_ref, b_ref, o_ref, xn_ref, carry_ref = refs
        xn = _rms(x_ref[...], g_ref[...]).astype(BF16)
        xn_ref[...] = xn
    else:
        x_ref, w_ref, b_ref, o_ref, carry_ref = refs
        xn = x_ref[...]

    @pl.when(pl.program_id(1) == 0)
    def _():
        carry_ref[...] = jnp.zeros_like(carry_ref)

    f = jnp.dot(xn, w_ref[...], preferred_element_type=F32) + b_ref[...]
    c = _log_sigmoid(f)
    row = lax.broadcasted_iota(jnp.int32, c.shape, 0)
    d = 1
    while d < tm:
        c = c + jnp.where(row >= d, pltpu.roll(c, d, 0), 0.0)
        d *= 2
    c = c + carry_ref[0:1, :]
    carry_ref[...] = jnp.broadcast_to(c[tm - 1:tm, :], carry_ref.shape)

    x = c * (-LOG2E)
    hi = x.astype(BF16).astype(F32)
    rest = x - hi
    mid = rest.astype(BF16).astype(F32)
    lo = rest - mid
    lane = lax.broadcasted_iota(jnp.int32, c.shape, 1)
    parts = jnp.where(lane < HEADS, hi,
                      jnp.where(lane < 2 * HEADS, pltpu.roll(mid, HEADS, 1),
                                jnp.where(lane < 3 * HEADS, pltpu.roll(lo, 2 * HEADS, 1), 0.0)))
    o_ref[0] = parts.astype(BF16)


def _fgate(x, w_f, b_f, batch, seq, g_pre=None, tm=512):
    t, k = x.shape
    nt = seq // tm
    normalize = g_pre is not None
    rows = pl.BlockSpec((tm, k), lambda b, s: (b * nt + s, 0))
    in_specs = [rows] + ([pl.BlockSpec((1, k), lambda b, s: (0, 0))] if normalize else []) + [
        pl.BlockSpec((k, LANES), lambda b, s: (0, 0)),
        pl.BlockSpec((1, LANES), lambda b, s: (0, 0))]
    out_specs = [pl.BlockSpec((1, tm, LANES), lambda b, s: (b, s, 0))]
    out_shape = [jax.ShapeDtypeStruct((batch, seq, LANES), BF16)]
    if normalize:
        out_specs.append(rows)
        out_shape.append(jax.ShapeDtypeStruct((t, k), BF16))
    args = (x, g_pre.reshape(1, k), w_f, b_f) if normalize else (x, w_f, b_f)
    outs = pl.pallas_call(
        functools.partial(_fgate_kernel, tm=tm, normalize=normalize),
        grid=(batch, nt),
        in_specs=in_specs,
        out_specs=out_specs,
        out_shape=out_shape,
        scratch_shapes=[pltpu.VMEM((SUBLANES, LANES), F32)],
        compiler_params=_params("arbitrary", "arbitrary"),
        name="fgate",
    )(*args)
    return tuple(outs) if normalize else outs[0]


def _attn_kernel(qa_ref, qb_ref, k_ref, v_ref, c_ref, o_ref, vt_ref, q_scr, m_ref, acc_ref,
                 *, blk, nb, group):
    hg, ip = pl.program_id(1), pl.program_id(2)
    head_cols = [slice(hh * HEAD_DIM, (hh + 1) * HEAD_DIM) for hh in range(group)]

    @pl.when(ip == 0)
    def _():
        for hh in range(group):
            for r in range(nb):
                vblk = v_ref[0, r * blk:(r + 1) * blk, head_cols[hh]]
                vt_ref[hh * nb + r, 0:HEAD_DIM, :] = vblk.astype(F32).T.astype(BF16)
                vt_ref[hh * nb + r, HEAD_DIM:HEAD_DIM + DEN_ROWS, :] = jnp.ones((DEN_ROWS, blk),
                                                                                BF16)

    m_ref[...] = jnp.full(m_ref.shape, -jnp.inf, F32)
    acc_ref[...] = jnp.zeros(acc_ref.shape, F32)

    lane = lax.broadcasted_iota(jnp.int32, (blk, HEAD_DIM), 1)
    for hh in range(group):
        head = hg * group + hh
        gate_lanes = (lane == head) | (lane == head + HEADS) | (lane == head + 2 * HEADS)
        ones = jnp.where(gate_lanes, 1.0, 0.0).astype(BF16)
        for tile, q_ref in enumerate((qa_ref, qb_ref)):
            q_scr[group * tile + hh] = jnp.concatenate([q_ref[0, :, head_cols[hh]], ones], axis=1)

    def item(t):
        if t == 0:
            return 0, ip, True
        if t == nb:
            return 1, nb - 1 - ip, True
        if t >= nb // 2:
            return 1, t - 1 - ip, False
        in_b = t > ip
        return in_b.astype(jnp.int32), jnp.where(in_b, t - 1 - ip, t - 1), False

    half = blk // 2
    halves = (slice(0, half), slice(half, blk))

    def nt_dot(a, b):
        return lax.dot_general(a, b, (((1,), (1,)), ((), ())), preferred_element_type=F32)

    def scores(t, heads):
        tile, kb, diag = item(t)
        ks = pl.multiple_of(kb * blk, blk)
        gate = c_ref[0, pl.ds(ks, blk), :]
        out = {}
        for hh in heads:
            k_aug = jnp.concatenate([k_ref[0, pl.ds(ks, blk), head_cols[hh]], gate], axis=1)
            q_aug = q_scr[group * tile + hh]
            if not diag:
                s = nt_dot(k_aug, q_aug)
                out[hh] = ((s,), (jnp.max(s, axis=0, keepdims=True),))
                continue
            k_lo, k_hi = k_aug[0:half], k_aug[half:blk]
            q_lo, q_hi = q_aug[0:half], q_aug[half:blk]
            key = lax.broadcasted_iota(jnp.int32, (half, half), 0)
            qry = lax.broadcasted_iota(jnp.int32, (half, half), 1)
            first = jnp.where(key <= qry, nt_dot(k_lo, q_lo), -jnp.inf)
            above = nt_dot(k_lo, q_hi)
            last = jnp.where(key <= qry, nt_dot(k_hi, q_hi), -jnp.inf)
            right = jnp.maximum(jnp.max(above, axis=0, keepdims=True),
                                jnp.max(last, axis=0, keepdims=True))
            out[hh] = ((first, above, last), (jnp.max(first, axis=0, keepdims=True), right))
        return out

    def softmax_pv(t, scored):
        tile, kb, _ = item(t)
        for hh in scored:
            idx = group * tile + hh
            pieces, bmax = scored[hh]
            if len(pieces) == 1:
                m_old = m_ref[idx]
                m_new = jnp.maximum(m_old, bmax[0])
                alpha = jnp.exp2(m_old - m_new)
                p = jnp.exp2(pieces[0] - m_new)
                pv = jnp.dot(vt_ref[hh * nb + kb], p.astype(BF16), preferred_element_type=F32)
                acc_ref[idx] = alpha * acc_ref[idx] + pv
                m_ref[idx] = m_new
                continue
            first, above, last = pieces
            for side, cols in enumerate(halves):
                m_old = m_ref[idx, :, cols]
                m_new = jnp.maximum(m_old, bmax[side])
                alpha = jnp.exp2(m_old - m_new)
                if side == 0:
                    p = jnp.exp2(first - m_new).astype(BF16)
                    pv = jnp.dot(vt_ref[hh * nb + kb, :, 0:half], p, preferred_element_type=F32)
                else:
                    p = jnp.concatenate([jnp.exp2(above - m_new), jnp.exp2(last - m_new)],
                                        axis=0).astype(BF16)
                    pv = jnp.dot(vt_ref[hh * nb + kb], p, preferred_element_type=F32)
                acc_ref[idx, :, cols] = alpha * acc_ref[idx, :, cols] + pv
                m_ref[idx, :, cols] = m_new

    pending = [scores(t, range(group)) for t in range(ATT_LOOKAHEAD)]
    for t in range(nb + 1):
        if t + ATT_LOOKAHEAD < nb + 1:
            pending.append(scores(t + ATT_LOOKAHEAD, range(group)))
        softmax_pv(t, pending.pop(0))

    for tile in range(2):
        for hh in range(group):
            acc = acc_ref[group * tile + hh]
            o = (acc[0:HEAD_DIM] / acc[HEAD_DIM:HEAD_DIM + 1]).T
            o_ref[0, 0, tile, :, head_cols[hh]] = o.astype(o_ref.dtype)


def _attention(qkv, gate, batch, seq, blk=ATT_BLOCK, group=ATT_GROUP):
    nb = seq // blk
    width = group * HEAD_DIM
    ngroup = HEADS // group
    return pl.pallas_call(
        functools.partial(_attn_kernel, blk=blk, nb=nb, group=group),
        grid=(batch, ngroup, nb // 2),
        in_specs=[
            pl.BlockSpec((1, blk, width), lambda b, h, i: (b, i, h)),
            pl.BlockSpec((1, blk, width), lambda b, h, i: (b, nb - 1 - i, h)),
            pl.BlockSpec((1, seq, width), lambda b, h, i: (b, 0, ngroup + h)),
            pl.BlockSpec((1, seq, width), lambda b, h, i: (b, 0, 2 * ngroup + h)),
            pl.BlockSpec((1, seq, LANES), lambda b, h, i: (b, 0, 0)),
        ],
        out_specs=pl.BlockSpec((1, 1, 2, blk, width), lambda b, h, i: (b, i, 0, 0, h)),
        out_shape=jax.ShapeDtypeStruct((batch, nb // 2, 2, blk, D_FOX), BF16),
        scratch_shapes=[pltpu.VMEM((group * nb, HEAD_DIM + DEN_ROWS, blk), BF16),
                        pltpu.VMEM((2 * group, blk, 2 * HEAD_DIM), BF16),
                        pltpu.VMEM((2 * group, 1, blk), F32),
                        pltpu.VMEM((2 * group, HEAD_DIM + DEN_ROWS, blk), F32)],
        compiler_params=_params("arbitrary", "arbitrary", "arbitrary"),
        name="fox_attention",
    )(qkv, qkv, qkv, qkv, gate)


def _block_middle_rows(cum, cum_ref, chunk):
    row = lax.broadcasted_iota(jnp.int32, cum.shape, 0)
    spread = lambda r: jnp.broadcast_to(cum_ref[r:r + 1, :], (SUBLANES, cum.shape[-1]))
    out = []
    h = chunk // 2
    while h >= 4:
        reps = max(2 * h // SUBLANES, 1)
        r = jnp.concatenate([spread(b + h - 1) for b in range(0, chunk, 2 * h)
                             for _ in range(reps)], axis=0)
        out.append((h, r))
        h //= 2
    low = jnp.concatenate([spread(b + 1) for b in range(0, chunk, SUBLANES)], axis=0)
    high = jnp.concatenate([spread(b + 5) for b in range(0, chunk, SUBLANES)], axis=0)
    out.append((2, jnp.where(row % SUBLANES < 4, low, high)))
    tiles = cum.reshape(chunk // SUBLANES, SUBLANES, cum.shape[-1])
    above = pltpu.roll(tiles, 1, 1).reshape(cum.shape)
    out.append((1, jnp.where(row % 2 == 1, above, cum)))
    return out


def _hgrn_kernel(q_ref, f_ref, i_ref, g_ref, wlb_ref, go_ref, o_ref, state_ref, cum_ref,
                 *, chunk, layer):
    @pl.when(pl.program_id(1) == 0)
    def _():
        state_ref[...] = jnp.zeros_like(state_ref)

    if layer > 0:
        w = wlb_ref[...]
        ew = jnp.exp(w - jnp.max(w, axis=0, keepdims=True))
        sm = ew / jnp.sum(ew, axis=0, keepdims=True)
        lb_all = sm[1:2]
        for dpt in range(2, layer + 1):
            lb_all = lb_all + sm[dpt:dpt + 1]
        log_lb_all = jnp.log(lb_all)
        log_1mlb_all = jnp.log1p(-lb_all)

    row = lax.broadcasted_iota(jnp.int32, (chunk, HEAD_DIM), 0)
    tile_row = lax.broadcasted_iota(jnp.int32, (chunk // SUBLANES, SUBLANES, HEAD_DIM), 1)
    rr = lax.broadcasted_iota(jnp.int32, (chunk, chunk), 0)
    cc = lax.broadcasted_iota(jnp.int32, (chunk, chunk), 1)
    pair_mask = {}
    side = {}
    h = chunk // 2
    while h >= 1:
        pair_mask[h] = (((rr // (2 * h)) == (cc // (2 * h)))
                        & (rr % (2 * h) >= h) & (cc % (2 * h) < h))
        side[h] = jnp.where(row % (2 * h) >= h, 1.0, -1.0)
        h //= 2

    def scores(rows):
        stage, heads = [], []
        for hd in range(HEADS):
            sl = slice(hd * HEAD_DIM, (hd + 1) * HEAD_DIM)
            q = q_ref[rows, sl]
            f = f_ref[rows, sl]
            v = i_ref[rows, sl]
            if layer == 0:
                log_f = _log_sigmoid(f)
                k = 1.0 / (1.0 + jnp.exp(f))
            else:
                a = log_lb_all[:, sl]
                b = log_1mlb_all[:, sl] + _log_sigmoid(f)
                log_f = jnp.maximum(a, b) + jnp.log(1.0 + jnp.exp(-jnp.abs(a - b)))
                k = (1.0 - lb_all[:, sl]) * (1.0 / (1.0 + jnp.exp(f)))

            cum = log_f * LOG2E
            tiles = cum.reshape(chunk // SUBLANES, SUBLANES, HEAD_DIM)
            d = 1
            while d < SUBLANES:
                tiles = tiles + jnp.where(tile_row >= d, pltpu.roll(tiles, d, 1), 0.0)
                d *= 2
            offset, offsets = None, [jnp.zeros((1, 1, HEAD_DIM), F32)]
            for tile in range(chunk // SUBLANES - 1):
                total = tiles[tile:tile + 1, SUBLANES - 1:SUBLANES, :]
                offset = total if offset is None else offset + total
                offsets.append(offset)
            cum = (tiles + jnp.concatenate(offsets, axis=0)).reshape(chunk, HEAD_DIM)
            last = cum[chunk - 1:chunk, :]

            cum_ref[hd] = cum
            heads.append((q.astype(BF16), k.astype(BF16), cum))
            stage.append([None, v.astype(BF16),
                          jnp.sum(q * k, axis=-1, keepdims=True) * v,
                          (q * jnp.exp2(cum)).astype(BF16),
                          (k * jnp.exp2(last - cum)).astype(BF16), jnp.exp2(last)])

        middles = [_block_middle_rows(cum, cum_ref.at[hd], chunk)
                   for hd, (_, _, cum) in enumerate(heads)]
        scs = [jnp.zeros((chunk, chunk), F32) for _ in heads]
        for lvl in range(len(middles[0])):
            for hd, (q16, k16, cum) in enumerate(heads):
                h, r = middles[hd][lvl]
                decay = jnp.exp2((cum - r) * side[h]).astype(BF16)
                s_l = lax.dot_general(q16 * decay, k16 * decay,
                                      (((1,), (1,)), ((), ())), preferred_element_type=F32)
                scs[hd] = jnp.where(pair_mask[h], s_l, scs[hd])
        for hd in range(HEADS):
            stage[hd][0] = scs[hd].astype(BF16)
        return stage

    def recur(rows, stage):
        for hd in range(HEADS):
            sl = slice(hd * HEAD_DIM, (hd + 1) * HEAD_DIM)
            sc16, v16, o_diag, qd, kd, decay_last = stage[hd]
            st = state_ref[hd]
            o = o_diag + jnp.dot(sc16, v16, preferred_element_type=F32)
            o = o + lax.dot_general(qd, st.astype(BF16), (((1,), (1,)), ((), ())),
                                    preferred_element_type=F32)
            upd = lax.dot_general(v16, kd, (((0,), (0,)), ((), ())),
                                  preferred_element_type=F32)
            state_ref[hd] = st * decay_last + upd

            g = g_ref[rows, sl]
            y = _rms(o, go_ref[...]) * (g * (1.0 / (1.0 + jnp.exp(-g))))
            o_ref[rows, sl] = y.astype(o_ref.dtype)

    n_chunks = q_ref.shape[0] // chunk
    rows = [slice(c * chunk, (c + 1) * chunk) for c in range(n_chunks)]
    pending = [scores(rows[c]) for c in range(min(HGRN_LOOKAHEAD, n_chunks))]
    for c in range(n_chunks):
        if c + HGRN_LOOKAHEAD < n_chunks:
            pending.append(scores(rows[c + HGRN_LOOKAHEAD]))
        recur(rows[c], pending.pop(0))


def _hgrn(zh, w_lb, g_o, layer, batch, seq, chunk=HGRN_CHUNK,
          chunks_per_step=HGRN_CHUNKS_PER_STEP):
    t = zh.shape[0]
    rows = chunk * chunks_per_step
    ns = seq // rows
    spec = lambda grp: pl.BlockSpec((rows, D_HGRN), lambda b, c: (b * ns + c, grp))
    return pl.pallas_call(
        functools.partial(_hgrn_kernel, chunk=chunk, layer=layer),
        grid=(batch, ns),
        in_specs=[spec(0), spec(1), spec(2), spec(3),
                  pl.BlockSpec(w_lb.shape, lambda b, c: (0, 0)),
                  pl.BlockSpec((1, HEAD_DIM), lambda b, c: (0, 0))],
        out_specs=pl.BlockSpec((rows, D_HGRN), lambda b, c: (b * ns + c, 0)),
        out_shape=jax.ShapeDtypeStruct((t, D_HGRN), BF16),
        scratch_shapes=[pltpu.VMEM((HEADS, HEAD_DIM, HEAD_DIM), F32),
                        pltpu.VMEM((HEADS, chunk, HEAD_DIM), F32)],
        compiler_params=_params("arbitrary", "arbitrary"),
        name="hgrn2",
    )(zh, zh, zh, zh, w_lb, g_o.reshape(1, HEAD_DIM))


def _residual_norm_store(branch, rows, h_ref, gpost_ref, gnext_ref, h_out_ref, hn_out_ref):
    h = h_ref[rows, :] + _rms(branch, gpost_ref[...])
    h_out_ref[rows, :] = h
    if hn_out_ref is not None:
        hn_out_ref[rows, :] = _rms(h, gnext_ref[...]).astype(hn_out_ref.dtype)


def _subtile_rows(tm):
    sub = tm // ROW_SUBTILES
    return [slice(s * sub, (s + 1) * sub) for s in range(ROW_SUBTILES)]


def _outproj_kernel(ya_ref, yb_ref, wa_ref, wb_ref, h_ref, gpost_ref, gnext_ref,
                    h_out_ref, hn_out_ref):
    mixes = []
    for rows in _subtile_rows(h_ref.shape[0]):
        mix = jnp.dot(ya_ref[rows, :], wa_ref[...], preferred_element_type=F32)
        mixes.append(mix + jnp.dot(yb_ref[rows, :], wb_ref[...], preferred_element_type=F32))
    for rows, mix in zip(_subtile_rows(h_ref.shape[0]), mixes):
        _residual_norm_store(mix, rows, h_ref, gpost_ref, gnext_ref, h_out_ref, hn_out_ref)


def _outproj(ya, yb, w_out, layer, h, g_post, g_next):
    t, d = h.shape
    _, half_nb, _, tm, ka = ya.shape
    nb = 2 * half_nb
    kb = yb.shape[1]
    row = lambda width: pl.BlockSpec((tm, width), lambda i: (i, 0))
    gain = _resident((1, d), lambda i: (0, 0))

    def ya_index(i):
        b, tile = i // nb, i % nb
        second = tile >= half_nb
        return (b, jnp.where(second, nb - 1 - tile, tile), second.astype(jnp.int32), 0, 0)

    return pl.pallas_call(
        _outproj_kernel,
        grid=(t // tm,),
        in_specs=[pl.BlockSpec((None, None, None, tm, ka), ya_index), row(kb),
                  _resident((None, ka, d), lambda i: (layer, 0, 0)),
                  _resident((None, kb, d), lambda i: (layer, ka // kb, 0)),
                  row(d), gain, gain],
        out_specs=[row(d), row(d)],
        out_shape=[jax.ShapeDtypeStruct((t, d), F32), jax.ShapeDtypeStruct((t, d), BF16)],
        compiler_params=_params("arbitrary"),
        name="outproj",
    )(ya, yb, w_out, w_out, h, g_post.reshape(1, d), g_next.reshape(1, d))


def _upproj_kernel(x_ref, wg_ref, wu_ref, cwg_ref, cwu_ref, cbg_ref, cbu_ref, o_ref,
                   wgb_ref, wub_ref, ug_ref, uu_ref, *, tm, tiles_per_seq):
    i = pl.program_id(1)

    @pl.when(i == 0)
    def _():
        wgb_ref[...] = wg_ref[...].astype(BF16)
        wub_ref[...] = wu_ref[...].astype(BF16)

    @pl.when(i % tiles_per_seq == 0)
    def _():
        ug_ref[0:HALO, :] = jnp.zeros((HALO, ug_ref.shape[1]), F32)
        uu_ref[0:HALO, :] = jnp.zeros((HALO, uu_ref.shape[1]), F32)

    x = x_ref[...]
    ug_ref[HALO:tm + HALO, :] = jnp.dot(x, wgb_ref[...], preferred_element_type=F32)
    uu_ref[HALO:tm + HALO, :] = jnp.dot(x, wub_ref[...], preferred_element_type=F32)

    def conv(u_ref, cw_ref, cb_ref):
        acc = cb_ref[...]
        for j in reversed(range(CONV_WIDTH)):
            back = CONV_WIDTH - 1 - j
            acc = acc + cw_ref[j:j + 1, :] * u_ref[HALO - back:tm + HALO - back, :]
        return acc

    gate = conv(ug_ref, cwg_ref, cbg_ref)
    up = conv(uu_ref, cwu_ref, cbu_ref)
    o_ref[...] = (gate * (1.0 / (1.0 + jnp.exp(-gate))) * up).astype(o_ref.dtype)

    ug_ref[0:HALO, :] = ug_ref[tm:tm + HALO, :]
    uu_ref[0:HALO, :] = uu_ref[tm:tm + HALO, :]


def _upproj(xn, w_up, conv_w, conv_b, layer, seq, tm=1024, tn=512):
    t, k = xn.shape
    nj = D_FF // tn
    lay = lambda r, off: pl.BlockSpec((None, r, tn), lambda j, i: (layer, 0, j + off))
    return pl.pallas_call(
        functools.partial(_upproj_kernel, tm=tm, tiles_per_seq=seq // tm),
        grid=(nj, t // tm),
        in_specs=[pl.BlockSpec((tm, k), lambda j, i: (i, 0)),
                  lay(k, 0), lay(k, nj),
                  lay(CONV_WIDTH, 0), lay(CONV_WIDTH, nj),
                  lay(1, 0), lay(1, nj)],
        out_specs=pl.BlockSpec((tm, tn), lambda j, i: (i, j)),
        out_shape=jax.ShapeDtypeStruct((t, D_FF), BF16),
        scratch_shapes=[pltpu.VMEM((k, tn), BF16), pltpu.VMEM((k, tn), BF16),
                        pltpu.VMEM((tm + HALO, tn), F32), pltpu.VMEM((tm + HALO, tn), F32)],
        compiler_params=_params("arbitrary", "arbitrary"),
        name="upproj_conv",
    )(xn, w_up, w_up, conv_w, conv_w, conv_b, conv_b)


def _downproj_kernel(a_ref, w_ref, h_ref, gpost_ref, gnext_ref, h_out_ref, hn_out_ref):
    ff = jnp.dot(a_ref[...], w_ref[...], preferred_element_type=F32)
    _residual_norm_store(ff, slice(None), h_ref, gpost_ref, gnext_ref, h_out_ref, hn_out_ref)


def _downproj(a, w_down, layer, h, g_post, g_next, tm=256):
    t, d = h.shape
    k = a.shape[1]
    row = lambda width: pl.BlockSpec((tm, width), lambda i: (i, 0))
    gain = _resident((1, d), lambda i: (0, 0))
    return pl.pallas_call(
        _downproj_kernel,
        grid=(t // tm,),
        in_specs=[row(k), _resident((None, k, d), lambda i: (layer, 0, 0)), row(d), gain, gain],
        out_specs=[row(d), row(d)],
        out_shape=[jax.ShapeDtypeStruct((t, d), F32), jax.ShapeDtypeStruct((t, d), BF16)],
        compiler_params=_params("arbitrary"),
        name="downproj",
    )(a, w_down, h, g_post.reshape(1, d), g_next.reshape(1, d))


def _ple_kernel(p_ref, hn_ref, wp_ref, wg_ref, h_ref, gpost_ref, gnext_ref,
                h_out_ref, hn_out_ref=None):
    pairs = []
    for rows in _subtile_rows(h_ref.shape[0]):
        e = jnp.dot(p_ref[rows, :].astype(BF16), wp_ref[...], preferred_element_type=F32)
        z = jnp.dot(hn_ref[rows, :], wg_ref[...], preferred_element_type=F32)
        pairs.append((e, z))
    for rows, (e, z) in zip(_subtile_rows(h_ref.shape[0]), pairs):
        gated = e * (1.0 / (1.0 + jnp.exp(-z)))
        _residual_norm_store(gated, rows, h_ref, gpost_ref, gnext_ref, h_out_ref, hn_out_ref)


def _ple(p, hn, w_proj, w_gate, layer, h, g_post, g_next, tm=512):
    t, d = h.shape
    kp = p.shape[1]
    row = lambda width: pl.BlockSpec((tm, width), lambda i: (i, 0))
    gain = _resident((1, d), lambda i: (0, 0))
    last = g_next is None
    out_shape = [jax.ShapeDtypeStruct((t, d), F32)]
    if not last:
        out_shape.append(jax.ShapeDtypeStruct((t, d), BF16))
    outs = pl.pallas_call(
        _ple_kernel,
        grid=(t // tm,),
        in_specs=[row(kp), row(d),
                  _resident((None, kp, d), lambda i: (layer, 0, 0)),
                  _resident((None, d, d), lambda i: (layer, 0, 0)),
                  row(d), gain, gain],
        out_specs=[row(d)] * len(out_shape),
        out_shape=out_shape,
        compiler_params=_params("arbitrary"),
        name="ple",
    )(p, hn, w_proj, w_gate, h, g_post.reshape(1, d),
      (g_post if last else g_next).reshape(1, d))
    return (outs[0], None) if last else tuple(outs)


def kernel(x, p, g_mix_pre, w_in, b_fox_f, w_hgrn_lb, g_hgrn_out, w_out, g_mix_post,
           g_ffn_pre, w_up, conv_w, conv_b, w_down, g_ffn_post,
           g_ple_in, w_ple_gate, w_ple_proj, g_ple_post):
    batch, seq, d = x.shape
    depth = w_in.shape[0]
    t = batch * seq
    off_f = 3 * D_FOX
    off_hg = off_f + HEADS

    qscale = jnp.concatenate([jnp.full((1, D_FOX), LOG2E * HEAD_DIM ** -0.5, F32),
                              jnp.ones((1, 2 * D_FOX), F32)], axis=1)
    ones_hg = jnp.ones((1, 4 * D_HGRN), F32)
    w_in_t = jnp.swapaxes(w_in, 1, 2)
    w_f = jnp.pad(w_in[:, :, off_f:off_hg], ((0, 0), (0, 0), (0, LANES - HEADS))).astype(BF16)
    b_f = jnp.pad(b_fox_f, ((0, 0), (0, LANES - HEADS)))
    conv_b3 = conv_b.reshape(depth, 1, -1)
    w_out16, w_down16 = w_out.astype(BF16), w_down.astype(BF16)
    w_gate16, w_pproj16 = w_ple_gate.astype(BF16), w_ple_proj.astype(BF16)

    h = x.reshape(t, d)
    hn = None
    for i in range(depth):
        if hn is None:
            c3, hn = _fgate(h, w_f[i], b_f[i:i + 1], batch, seq, g_pre=g_mix_pre[i])
        else:
            c3 = _fgate(hn, w_f[i], b_f[i:i + 1], batch, seq)
        qkv = _proj(hn, w_in_t, i, 0, 3 * D_FOX, qscale, BF16, name="inproj_qkv")
        zh = _proj(hn, w_in_t, i, off_hg, 4 * D_HGRN, ones_hg, F32, name="inproj_hgrn")
        y_fox = _attention(qkv.reshape(batch, seq, 3 * D_FOX), c3, batch, seq)
        y_hg = _hgrn(zh, w_hgrn_lb, g_hgrn_out[i], i, batch, seq)

        h, hn = _outproj(y_fox, y_hg, w_out16, i, h, g_mix_post[i], g_ffn_pre[i])
        a = _upproj(hn, w_up, conv_w, conv_b3, i, seq)
        h, hn = _downproj(a, w_down16, i, h, g_ffn_post[i], g_ple_in[i])
        g_next = g_mix_pre[i + 1] if i + 1 < depth else None
        h, hn = _ple(p[i].reshape(t, -1), hn, w_pproj16, w_gate16, i, h, g_ple_post[i], g_next)
    return h.reshape(batch, seq, d)
```
